```python
import jax
import jax.numpy as jnp
from jax import lax
import numpy as np

D_MODEL = 1024
BATCH = 2
SEQ = 8192
DEPTH = 4
DEC_BATCH = 128
DEC_SEQ = 1
PAST_LEN = 2048
PAGE_SIZE = 128

HEAD_DIM = 64
N_HEADS = D_MODEL // HEAD_DIM
NSA_KV_HEADS = N_HEADS // 4
NSA_GROUP = N_HEADS // NSA_KV_HEADS
ROT_DIM = HEAD_DIM // 4
ROPE_THETA = 500000.0
CMP_BLOCK = 64
SEL_BLOCK = 64
TOP_N = 16
WINDOW = 512
N_BRANCH = 3
D_FF = 2816
Q_BLOCK = 128
N_MIXERS = 3
N_NSA = len(range(0, DEPTH, N_MIXERS))
N_FOX = len(range(1, DEPTH, N_MIXERS))
N_SB = len(range(2, DEPTH, N_MIXERS))
NORM_EPS = 1e-6
FORCE_SCORE = 1e4
NEG_INF = -1e30
NSA_QD = N_HEADS * HEAD_DIM
NSA_KVD = NSA_KV_HEADS * HEAD_DIM
NSA_IN = NSA_QD + 6 * NSA_KVD + N_BRANCH * N_HEADS
FOX_IN = 3 * N_HEADS * HEAD_DIM + N_HEADS
SB_IN = 3 * N_HEADS * HEAD_DIM

kernel_name = 'hybrid_nsa_fox_stickbreak_decode_step'


def rmsnorm(x, g):
    xf = x.astype(jnp.float32)
    y = xf * lax.rsqrt(jnp.mean(xf * xf, axis=-1, keepdims=True) + NORM_EPS)
    return (y * g.astype(jnp.float32)).astype(x.dtype)


def rope_partial(x, pos):
    half = ROT_DIM // 2
    inv_freq = ROPE_THETA ** (-jnp.arange(half, dtype=jnp.float32) * (2.0 / ROT_DIM))
    ang = pos.astype(jnp.float32)[:, None] * inv_freq[None, :]
    shape = (pos.shape[0],) + (1,) * (x.ndim - 3) + (half,)
    cos = jnp.cos(ang).reshape(shape)
    sin = jnp.sin(ang).reshape(shape)
    xf = x.astype(jnp.float32)
    x1 = xf[..., :half]
    x2 = xf[..., half:ROT_DIM]
    out = jnp.concatenate([x1 * cos - x2 * sin, x1 * sin + x2 * cos, xf[..., ROT_DIM:]], axis=-1)
    return out.astype(x.dtype)


def masked_softmax(logits, mask):
    z = jnp.where(mask, logits.astype(jnp.float32), NEG_INF)
    return jax.nn.softmax(z, axis=-1) * mask.astype(jnp.float32)


def ffn_half(x, g, w_gate, w_up, w_down):
    h = rmsnorm(x, g)
    return x + 0.5 * ((jax.nn.silu(h @ w_gate) * (h @ w_up)) @ w_down)


def flatten_pages(p):
    return p.reshape((p.shape[0], p.shape[1] * p.shape[2]) + p.shape[3:])


def query_blocks(fn, n_q):
    def body(b):
        s0 = b * Q_BLOCK
        return fn(s0, s0 + jnp.arange(Q_BLOCK, dtype=jnp.int32))
    o = lax.map(body, jnp.arange(n_q // Q_BLOCK, dtype=jnp.int32))
    o = jnp.moveaxis(o, 0, 1)
    return o.reshape((o.shape[0], o.shape[1] * o.shape[2]) + o.shape[3:])


def nsa_project(h, pos, w_in, q_gain, k_gain):
    n, t, _ = h.shape
    z = h @ w_in
    q = z[..., :NSA_QD].reshape(n, t, N_HEADS, HEAD_DIM)
    kv = z[..., NSA_QD:NSA_QD + 6 * NSA_KVD].reshape(n, t, N_BRANCH, 2, NSA_KV_HEADS, HEAD_DIM)
    gates = jax.nn.sigmoid(z[..., NSA_QD + 6 * NSA_KVD:].astype(jnp.float32)).reshape(n, t, N_HEADS, N_BRANCH)
    q = rope_partial(rmsnorm(q, q_gain), pos)
    k = rope_partial(rmsnorm(kv[:, :, :, 0], k_gain), pos)
    kv = jnp.stack([k, kv[:, :, :, 1]], axis=3).reshape(n, t, 2 * N_BRANCH, NSA_KV_HEADS, HEAD_DIM)
    return q, gates, kv[:, :, :4], kv[:, :, 4:]


def nsa_compress(rows_cmp, cmp_pos, w1, w2):
    n, length = rows_cmp.shape[:2]
    nc = length // CMP_BLOCK
    blk = rows_cmp[:, :nc * CMP_BLOCK].reshape(n, nc, CMP_BLOCK, 2, NSA_KV_HEADS, HEAD_DIM)
    blk = blk + jnp.transpose(cmp_pos, (1, 0, 2))[:, :, None, :]
    flat = jnp.transpose(blk, (0, 1, 3, 4, 2, 5)).reshape(n, nc, 2, NSA_KV_HEADS, CMP_BLOCK * HEAD_DIM)
    hid = jax.nn.silu(jnp.einsum('nbchf,cfe->nbche', flat, w1))
    out = jnp.einsum('nbche,ced->nbchd', hid, w2)
    return out[:, :, 0], out[:, :, 1]


def nsa_sel_blocks(rows_sel):
    n, length = rows_sel.shape[:2]
    ns = -(-length // SEL_BLOCK)
    r = jnp.pad(rows_sel, ((0, 0), (0, ns * SEL_BLOCK - length), (0, 0), (0, 0), (0, 0)))
    r = jnp.transpose(r.reshape(n, ns, SEL_BLOCK, 2, NSA_KV_HEADS, HEAD_DIM), (3, 0, 4, 1, 2, 5))
    return r[0], r[1]


def nsa_attend(q, gates, q_pos, kc, vc, ks, vs, kw, vw, w_pos):
    n, tq = q.shape[:2]
    scale = HEAD_DIM ** -0.5
    qg = q.reshape(n, tq, NSA_KV_HEADS, NSA_GROUP, HEAD_DIM)
    nc = kc.shape[1]
    c_end = (jnp.arange(nc, dtype=jnp.int32) + 1) * CMP_BLOCK - 1
    c_mask = c_end[None, :] <= q_pos[:, None]
    p_c = masked_softmax(jnp.einsum('nqhgd,nchd->nhgqc', qg, kc) * scale, c_mask)
    o_c = jnp.einsum('nhgqc,nchd->nqhgd', p_c, vc)
    ns = ks.shape[2]
    n_sel = min(TOP_N, ns)
    imp = jnp.pad(p_c.sum(axis=2), ((0, 0), (0, 0), (0, 0), (0, ns - nc)))
    blk = jnp.arange(ns, dtype=jnp.int32)[None, :]
    cur = (q_pos // SEL_BLOCK)[:, None]
    valid = blk <= cur
    forced = valid & ((blk == 0) | (blk == cur) | (blk == cur - 1))
    score = jnp.where(forced, FORCE_SCORE, jnp.where(valid, imp, NEG_INF))
    top_val, top_idx = lax.top_k(score, n_sel)
    ni = jnp.arange(n)[:, None, None, None]
    hi = jnp.arange(NSA_KV_HEADS)[None, :, None, None]
    k_sel = ks[ni, hi, top_idx]
    v_sel = vs[ni, hi, top_idx]
    k_pos = top_idx[..., None] * SEL_BLOCK + jnp.arange(SEL_BLOCK, dtype=jnp.int32)
    s_mask = (top_val > 0.5 * NEG_INF)[..., None] & (k_pos <= q_pos[None, None, :, None, None])
    s_logits = jnp.einsum('nqhgd,nhqsbd->nhgqsb', qg, k_sel) * scale
    p_s = masked_softmax(s_logits.reshape(n, NSA_KV_HEADS, NSA_GROUP, tq, n_sel * SEL_BLOCK),
                         s_mask.reshape(n, NSA_KV_HEADS, 1, tq, n_sel * SEL_BLOCK))
    o_s = jnp.einsum('nhgqsb,nhqsbd->nqhgd', p_s.reshape(n, NSA_KV_HEADS, NSA_GROUP, tq, n_sel, SEL_BLOCK), v_sel)
    dist = q_pos[:, None] - w_pos[None, :]
    w_mask = (dist >= 0) & (dist <= WINDOW) & (w_pos[None, :] >= 0)
    p_w = masked_softmax(jnp.einsum('nqhgd,nwhd->nhgqw', qg, kw) * scale, w_mask)
    o_w = jnp.einsum('nhgqw,nwhd->nqhgd', p_w, vw)
    g = gates.reshape(n, tq, NSA_KV_HEADS, NSA_GROUP, N_BRANCH)
    o = g[..., 0:1] * o_c + g[..., 1:2] * o_s + g[..., 2:3] * o_w
    return o.reshape(n, tq, N_HEADS * HEAD_DIM)


def nsa_prompt(h, w_in, q_gain, k_gain, cmp_pos, cmp_w1, cmp_w2, w_out):
    n, s, _ = h.shape
    pos = jnp.arange(s, dtype=jnp.int32)
    q, gates, rows, win = nsa_project(h, pos, w_in, q_gain, k_gain)
    kc, vc = nsa_compress(rows[:, :, 0:2], cmp_pos, cmp_w1, cmp_w2)
    ks, vs = nsa_sel_blocks(rows[:, :, 2:4])
    win_pad = jnp.pad(win, ((0, 0), (WINDOW, 0), (0, 0), (0, 0), (0, 0)))

    def block(s0, qp):
        wb = lax.dynamic_slice_in_dim(win_pad, s0, WINDOW + Q_BLOCK, axis=1)
        wpos = s0 - WINDOW + jnp.arange(WINDOW + Q_BLOCK, dtype=jnp.int32)
        return nsa_attend(lax.dynamic_slice_in_dim(q, s0, Q_BLOCK, axis=1),
                          lax.dynamic_slice_in_dim(gates, s0, Q_BLOCK, axis=1),
                          qp, kc, vc, ks, vs, wb[:, :, 0], wb[:, :, 1], wpos)

    o = query_blocks(block, s)
    y = o.astype(h.dtype) @ w_out
    return y, rows, win[:, s - min(WINDOW, s):]


def nsa_sample(h, cache_kv, win_buf, page_table, layer, w_in, q_gain, k_gain, cmp_pos, cmp_w1, cmp_w2, w_out):
    n, t, _ = h.shape
    past_c = flatten_pages(cache_kv[layer, page_table, :, 0:2])
    past_s = flatten_pages(cache_kv[layer, page_table, :, 2:4])
    past_len = past_c.shape[1]
    pos = past_len + jnp.arange(t, dtype=jnp.int32)
    q, gates, rows_new, win_new = nsa_project(h, pos, w_in, q_gain, k_gain)
    rows_c = jnp.concatenate([past_c.astype(rows_new.dtype), rows_new[:, :, 0:2]], axis=1)
    rows_s = jnp.concatenate([past_s.astype(rows_new.dtype), rows_new[:, :, 2:4]], axis=1)
    kc, vc = nsa_compress(rows_c, cmp_pos, cmp_w1, cmp_w2)
    ks, vs = nsa_sel_blocks(rows_s)
    wlen = win_buf.shape[1]
    win_all = jnp.concatenate([win_buf.astype(win_new.dtype), win_new], axis=1)
    wpos = past_len - wlen + jnp.arange(wlen + t, dtype=jnp.int32)
    o = nsa_attend(q, gates, pos, kc, vc, ks, vs, win_all[:, :, 0], win_all[:, :, 1], wpos)
    y = o.astype(h.dtype) @ w_out
    return y, rows_new, win_all[:, t:]


def fox_project(h, w_in, b_f, q_gain, k_gain):
    n, t, _ = h.shape
    hd = N_HEADS * HEAD_DIM
    z = h @ w_in
    qkv = z[..., :3 * hd].reshape(n, t, 3, N_HEADS, HEAD_DIM)
    q = rmsnorm(qkv[:, :, 0], q_gain)
    k = rmsnorm(qkv[:, :, 1], k_gain)
    logf = jax.nn.log_sigmoid(z[..., 3 * hd:].astype(jnp.float32) + b_f.astype(jnp.float32))
    return q, k, qkv[:, :, 2], logf


def fox_attend(q, q_pos, k, v, k_pos, cq, ck):
    logits = jnp.einsum('nqhd,nkhd->nhqk', q, k).astype(jnp.float32) * HEAD_DIM ** -0.5
    logits = logits + jnp.transpose(cq, (0, 2, 1))[..., None] - jnp.transpose(ck, (0, 2, 1))[:, :, None, :]
    p = masked_softmax(logits, k_pos[None, :] <= q_pos[:, None])
    return jnp.einsum('nhqk,nkhd->nqhd', p, v)


def fox_prompt(h, w_in, b_f, q_gain, k_gain, w_out):
    n, s, _ = h.shape
    q, k, v, logf = fox_project(h, w_in, b_f, q_gain, k_gain)
    c = jnp.cumsum(logf, axis=1)
    k_pos = jnp.arange(s, dtype=jnp.int32)

    def block(s0, qp):
        return fox_attend(lax.dynamic_slice_in_dim(q, s0, Q_BLOCK, axis=1), qp, k, v, k_pos,
                          lax.dynamic_slice_in_dim(c, s0, Q_BLOCK, axis=1), c)

    o = query_blocks(block, s).reshape(n, s, N_HEADS * HEAD_DIM)
    y = o.astype(h.dtype) @ w_out
    return y, jnp.stack([k, v], axis=2), logf


def fox_sample(h, cache_kv, cache_logf, page_table, layer, w_in, b_f, q_gain, k_gain, w_out):
    n, t, _ = h.shape
    k_past = flatten_pages(cache_kv[layer, page_table, :, 0])
    v_past = flatten_pages(cache_kv[layer, page_table, :, 1])
    lf_past = flatten_pages(cache_logf[layer, page_table])
    past_len = k_past.shape[1]
    pos = past_len + jnp.arange(t, dtype=jnp.int32)
    q, k, v, logf = fox_project(h, w_in, b_f, q_gain, k_gain)
    k_all = jnp.concatenate([k_past.astype(k.dtype), k], axis=1)
    v_all = jnp.concatenate([v_past.astype(v.dtype), v], axis=1)
    c = jnp.cumsum(jnp.concatenate([lf_past.astype(jnp.float32), logf], axis=1), axis=1)
    o = fox_attend(q, pos, k_all, v_all, jnp.arange(past_len + t, dtype=jnp.int32), c[:, past_len:], c)
    y = o.reshape(n, t, N_HEADS * HEAD_DIM).astype(h.dtype) @ w_out
    return y, jnp.stack([k, v], axis=2), logf


def sb_project(h, w_in):
    n, t, _ = h.shape
    qkv = (h @ w_in).reshape(n, t, 3, N_HEADS, HEAD_DIM)
    return qkv[:, :, 0], qkv[:, :, 1], qkv[:, :, 2]


def sb_attend(q, q_pos, k, v, k_pos):
    z = jnp.einsum('nqhd,nkhd->nhqk', q, k).astype(jnp.float32) * HEAD_DIM ** -0.5
    mask = k_pos[None, :] < q_pos[:, None]
    log_stay = jnp.where(mask, jax.nn.log_sigmoid(-z), 0.0)
    later = lax.cumsum(log_stay, axis=z.ndim - 1, reverse=True) - log_stay
    a = jnp.where(mask, jnp.exp(jax.nn.log_sigmoid(z) + later), 0.0)
    return jnp.einsum('nhqk,nkhd->nqhd', a, v)


def sb_prompt(h, w_in, w_out):
    n, s, _ = h.shape
    q, k, v = sb_project(h, w_in)
    k_pos = jnp.arange(s, dtype=jnp.int32)

    def block(s0, qp):
        return sb_attend(lax.dynamic_slice_in_dim(q, s0, Q_BLOCK, axis=1), qp, k, v, k_pos)

    o = query_blocks(block, s).reshape(n, s, N_HEADS * HEAD_DIM)
    y = o.astype(h.dtype) @ w_out
    return y, jnp.stack([k, v], axis=2)


def sb_sample(h, cache_kv, page_table, layer, w_in, w_out):
    n, t, _ = h.shape
    k_past = flatten_pages(cache_kv[layer, page_table, :, 0])
    v_past = flatten_pages(cache_kv[layer, page_table, :, 1])
    past_len = k_past.shape[1]
    pos = past_len + jnp.arange(t, dtype=jnp.int32)
    q, k, v = sb_project(h, w_in)
    k_all = jnp.concatenate([k_past.astype(k.dtype), k], axis=1)
    v_all = jnp.concatenate([v_past.astype(v.dtype), v], axis=1)
    o = sb_attend(q, pos, k_all, v_all, jnp.arange(past_len + t, dtype=jnp.int32))
    y = o.reshape(n, t, N_HEADS * HEAD_DIM).astype(h.dtype) @ w_out
    return y, jnp.stack([k, v], axis=2)


def setup_inputs(seed: int = 0) -> dict:
    key = jax.random.key(seed)
    k = jax.random.split(key, 27)
    n_pages = PAST_LEN // PAGE_SIZE
    used = DEC_BATCH * n_pages
    n_pool = used + max(1, used // 4)
    win_buf = min(WINDOW, PAST_LEN)
    hdim = N_HEADS * HEAD_DIM

    def nrm(kk, shape, scale=1.0):
        return scale * jax.random.normal(kk, shape, jnp.float32)

    def gain(kk, shape):
        return 1.0 + 0.05 * jax.random.normal(kk, shape, jnp.float32)

    page_table = jax.random.permutation(k[7], n_pool)[:used].reshape(DEC_BATCH, n_pages).astype(jnp.int32)
    return {
        'x_prompt': nrm(k[0], (BATCH, SEQ, D_MODEL)),
        'x_sample': nrm(k[1], (DEC_BATCH, DEC_SEQ, D_MODEL)),
        'cache_nsa_kv': nrm(k[2], (N_NSA, n_pool, PAGE_SIZE, 4, NSA_KV_HEADS, HEAD_DIM)),
        'state_nsa_win': nrm(k[3], (N_NSA, DEC_BATCH, win_buf, 2, NSA_KV_HEADS, HEAD_DIM)),
        'cache_fox_kv': nrm(k[4], (N_FOX, n_pool, PAGE_SIZE, 2, N_HEADS, HEAD_DIM)),
        'cache_fox_logf': jax.nn.log_sigmoid(nrm(k[5], (N_FOX, n_pool, PAGE_SIZE, N_HEADS)) + 2.5),
        'cache_sb_kv': nrm(k[6], (N_SB, n_pool, PAGE_SIZE, 2, N_HEADS, HEAD_DIM)),
        'page_table': page_table,
        'ffn_norm': gain(k[8], (DEPTH, 2, D_MODEL)),
        'ffn_w_gate': nrm(k[9], (DEPTH, 2, D_MODEL, D_FF), D_MODEL ** -0.5),
        'ffn_w_up': nrm(k[10], (DEPTH, 2, D_MODEL, D_FF), D_MODEL ** -0.5),
        'ffn_w_down': nrm(k[11], (DEPTH, 2, D_FF, D_MODEL), D_FF ** -0.5),
        'mix_norm': gain(k[12], (DEPTH, D_MODEL)),
        'nsa_w_in': nrm(k[13], (N_NSA, D_MODEL, NSA_IN), D_MODEL ** -0.5),
        'nsa_q_gain': gain(k[14], (N_NSA, HEAD_DIM)),
        'nsa_k_gain': gain(k[15], (N_NSA, HEAD_DIM)),
        'nsa_cmp_pos': nrm(k[16], (N_NSA, 2, CMP_BLOCK, HEAD_DIM), 0.1),
        'nsa_cmp_w1': nrm(k[17], (N_NSA, 2, CMP_BLOCK * HEAD_DIM, HEAD_DIM), (CMP_BLOCK * HEAD_DIM) ** -0.5),
        'nsa_cmp_w2': nrm(k[18], (N_NSA, 2, HEAD_DIM, HEAD_DIM), 2.0 * HEAD_DIM ** -0.5),
        'nsa_w_out': nrm(k[19], (N_NSA, hdim, D_MODEL), hdim ** -0.5),
        'fox_w_in': nrm(k[20], (N_FOX, D_MODEL, FOX_IN), D_MODEL ** -0.5),
        'fox_b_f': jax.random.uniform(k[21], (N_FOX, N_HEADS), jnp.float32, 1.0, 4.0),
        'fox_q_gain': gain(k[22], (N_FOX, HEAD_DIM)),
        'fox_k_gain': gain(k[23], (N_FOX, HEAD_DIM)),
        'fox_w_out': nrm(k[24], (N_FOX, hdim, D_MODEL), hdim ** -0.5),
        'sb_w_in': nrm(k[25], (N_SB, D_MODEL, SB_IN), D_MODEL ** -0.5),
        'sb_w_out': nrm(k[26], (N_SB, hdim, D_MODEL), hdim ** -0.5),
    }


def reference(x_prompt, x_sample, cache_nsa_kv, state_nsa_win, cache_fox_kv, cache_fox_logf, cache_sb_kv,
              page_table, ffn_norm, ffn_w_gate, ffn_w_up, ffn_w_down, mix_norm,
              nsa_w_in, nsa_q_gain, nsa_k_gain, nsa_cmp_pos, nsa_cmp_w1, nsa_cmp_w2, nsa_w_out,
              fox_w_in, fox_b_f, fox_q_gain, fox_k_gain, fox_w_out, sb_w_in, sb_w_out):
    xp, xs = x_prompt, x_sample
    nsa_kv_p, nsa_win_p, nsa_kv_s, nsa_win_s = [], [], [], []
    fox_kv_p, fox_lf_p, fox_kv_s, fox_lf_s = [], [], [], []
    sb_kv_p, sb_kv_s = [], []
    for i in range(DEPTH):
        j = i // N_MIXERS
        xp = ffn_half(xp, ffn_norm[i, 0], ffn_w_gate[i, 0], ffn_w_up[i, 0], ffn_w_down[i, 0])
        xs = ffn_half(xs, ffn_norm[i, 0], ffn_w_gate[i, 0], ffn_w_up[i, 0], ffn_w_down[i, 0])
        hp = rmsnorm(xp, mix_norm[i])
        hs = rmsnorm(xs, mix_norm[i])
        if i % N_MIXERS == 0:
            w = (nsa_w_in[j], nsa_q_gain[j], nsa_k_gain[j], nsa_cmp_pos[j], nsa_cmp_w1[j], nsa_cmp_w2[j], nsa_w_out[j])
            yp, kv_p, win_p = nsa_prompt(hp, *w)
            ys, kv_s, win_s = nsa_sample(hs, cache_nsa_kv, state_nsa_win[j], page_table, j, *w)
            nsa_kv_p.append(kv_p)
            nsa_win_p.append(win_p)
            nsa_kv_s.append(kv_s)
            nsa_win_s.append(win_s)
        elif i % N_MIXERS == 1:
            w = (fox_w_in[j], fox_b_f[j], fox_q_gain[j], fox_k_gain[j], fox_w_out[j])
            yp, kv_p, lf_p = fox_prompt(hp, *w)
            ys, kv_s, lf_s = fox_sample(hs, cache_fox_kv, cache_fox_logf, page_table, j, *w)
            fox_kv_p.append(kv_p)
            fox_lf_p.append(lf_p)
            fox_kv_s.append(kv_s)
            fox_lf_s.append(lf_s)
        else:
            yp, kv_p = sb_prompt(hp, sb_w_in[j], sb_w_out[j])
            ys, kv_s = sb_sample(hs, cache_sb_kv, page_table, j, sb_w_in[j], sb_w_out[j])
            sb_kv_p.append(kv_p)
            sb_kv_s.append(kv_s)
        xp = xp + yp
        xs = xs + ys
        xp = ffn_half(xp, ffn_norm[i, 1], ffn_w_gate[i, 1], ffn_w_up[i, 1], ffn_w_down[i, 1])
        xs = ffn_half(xs, ffn_norm[i, 1], ffn_w_gate[i, 1], ffn_w_up[i, 1], ffn_w_down[i, 1])
    return (xp, xs,
            jnp.stack(nsa_kv_p), jnp.stack(nsa_win_p), jnp.stack(fox_kv_p), jnp.stack(fox_lf_p), jnp.stack(sb_kv_p),
            jnp.stack(nsa_kv_s), jnp.stack(nsa_win_s), jnp.stack(fox_kv_s), jnp.stack(fox_lf_s), jnp.stack(sb_kv_s))
```

```python
import functools

import numpy as np
import jax
import jax.numpy as jnp
from jax import lax
from jax.experimental import pallas as pl
from jax.experimental.pallas import tpu as pltpu

F32 = jnp.float32
BF16 = jnp.bfloat16

D_MODEL = 1024
HEAD_DIM = 64
N_HEADS = 16
NSA_KV_HEADS = 4
NSA_GROUP = 4
ROT_DIM = 16
ROPE_THETA = 500000.0
CMP_BLOCK = 64
SEL_BLOCK = 64
TOP_N = 16
WINDOW = 512
NORM_EPS = 1e-6
FORCE_SCORE = 1e4
NEG_INF = -1e30
HD = N_HEADS * HEAD_DIM
NSA_KVD = NSA_KV_HEADS * HEAD_DIM
QK_SCALE = HEAD_DIM ** -0.5

LANE = 128
V7X_VMEM_BYTES = 64 * 1024 * 1024
VMEM_LIMIT = 56 * 1024 * 1024


def _cp(sem, vmem=VMEM_LIMIT):
    return pltpu.CompilerParams(dimension_semantics=sem, vmem_limit_bytes=vmem)


def _dot(a, b):
    return jnp.dot(a, b, preferred_element_type=F32)


def _nt_dot(a, b):
    return lax.dot_general(a, b, (((1,), (1,)), ((), ())), preferred_element_type=F32)


def _dot_x2(x, w):
    hi = x.astype(BF16)
    lo = (x - hi.astype(F32)).astype(BF16)
    return _dot(hi, w) + _dot(lo, w)


def _dot_x3(x, w):
    hi = x.astype(BF16)
    r = x - hi.astype(F32)
    mid = r.astype(BF16)
    lo = (r - mid.astype(F32)).astype(BF16)
    return _dot(hi, w) + _dot(mid, w) + _dot(lo, w)


def _dot_w3(w, x):
    hi = x.astype(BF16)
    r = x - hi.astype(F32)
    mid = r.astype(BF16)
    lo = (r - mid.astype(F32)).astype(BF16)
    return _dot(w, hi) + _dot(w, mid) + _dot(w, lo)


def _rmsnorm_rows(x, g):
    ms = jnp.mean(x * x, axis=-1, keepdims=True)
    return x * lax.rsqrt(ms + NORM_EPS) * g


def _log_sigmoid(x):
    return jnp.minimum(x, 0.0) - jnp.log1p(jnp.exp(-jnp.abs(x)))


def _const_spec(shape):
    nd = len(shape)
    return pl.BlockSpec(shape, lambda *_: (0,) * nd)


def _ffn_kernel(x_ref, g_ref, wg_ref, wu_ref, wd_ref, o_ref, *, ff_chunk):
    x = x_ref[...]
    h = _rmsnorm_rows(x, g_ref[...]).astype(BF16)
    n_ff = wg_ref.shape[1]
    acc = None
    for c in range(n_ff // ff_chunk):
        sl = slice(c * ff_chunk, (c + 1) * ff_chunk)
        a = _dot(h, wg_ref[:, sl])
        b = _dot(h, wu_ref[:, sl])
        u = (a * jax.nn.sigmoid(a) * b).astype(BF16)
        part = _dot(u, wd_ref[sl, :])
        acc = part if acc is None else acc + part
    o_ref[...] = x + 0.5 * acc


def _ffn_half(x, g, wg, wu, wd):
    m, d = x.shape
    ff = wg.shape[1]
    tm = 256 if m % 256 == 0 else m
    ff_chunk = ff // 2 if (ff // 2) % LANE == 0 else ff
    return pl.pallas_call(
        functools.partial(_ffn_kernel, ff_chunk=ff_chunk),
        grid=(m // tm,),
        in_specs=[
            pl.BlockSpec((tm, d), lambda i: (i, 0)),
            _const_spec((1, d)),
            _const_spec((d, ff)),
            _const_spec((d, ff)),
            _const_spec((ff, d)),
        ],
        out_specs=pl.BlockSpec((tm, d), lambda i: (i, 0)),
        out_shape=jax.ShapeDtypeStruct((m, d), F32),
        compiler_params=_cp(("parallel",)),
        name="ffn_half",
    )(x, g.reshape(1, d), wg, wu, wd)


def _outproj_kernel(x_ref, o_ref, w_ref, y_ref):
    y_ref[...] = x_ref[...] + _dot(o_ref[...].astype(BF16), w_ref[...])


def _out_proj(x, o, w):
    m, d = x.shape
    tm = 512 if m % 512 == 0 else m
    return pl.pallas_call(
        _outproj_kernel,
        grid=(m // tm,),
        in_specs=[
            pl.BlockSpec((tm, d), lambda i: (i, 0)),
            pl.BlockSpec((tm, o.shape[1]), lambda i: (i, 0)),
            _const_spec(w.shape),
        ],
        out_specs=pl.BlockSpec((tm, d), lambda i: (i, 0)),
        out_shape=jax.ShapeDtypeStruct((m, d), F32),
        compiler_params=_cp(("parallel",)),
        name="out_proj",
    )(x, o, w)


def _head_norm(x, gain, bd):
    ms = _dot_x2(x * x, bd)
    return x * lax.rsqrt(ms + NORM_EPS) * gain


def _rope(y, cos, s_lo, s_hi):
    return y * cos + pltpu.roll(y, 8, 1) * s_hi + pltpu.roll(y, LANE - 8, 1) * s_lo


def _nsa_proj_kernel(x_ref, g_ref, w_ref, qg_ref, kg_ref, cos_ref, slo_ref, shi_ref, bd_ref,
                     q_ref, rows_ref, win_ref, kvb_ref, gate_ref):
    h = _rmsnorm_rows(x_ref[...], g_ref[...]).astype(BF16)
    cos, s_lo, s_hi = cos_ref[...], slo_ref[...], shi_ref[...]
    bd = bd_ref[...]
    qg, kg = qg_ref[...], kg_ref[...]

    def chunk(c):
        return _dot(h, w_ref[:, c * LANE:(c + 1) * LANE])

    for c in range(HD // LANE):
        y = _rope(_head_norm(chunk(c), qg, bd), cos, s_lo, s_hi) * QK_SCALE
        q_ref[:, c * LANE:(c + 1) * LANE] = y.astype(q_ref.dtype)
    base = HD // LANE
    for b in range(3):
        for part in range(2):
            for cc in range(2):
                c = base + 4 * b + 2 * part + cc
                y = chunk(c)
                if part == 0:
                    y = _rope(_head_norm(y, kg, bd), cos, s_lo, s_hi)
                col = (2 * part + cc) * LANE
                if b < 2:
                    rows_ref[:, 2 * b * NSA_KVD + col - 0:2 * b * NSA_KVD + col + LANE] = y
                else:
                    win_ref[:, col:col + LANE] = y
                kcol = (4 * b + 2 * part + cc) * LANE
                kvb_ref[:, kcol:kcol + LANE] = y.astype(BF16)
    gate_ref[...] = jax.nn.sigmoid(chunk(base + 12))


def _fox_proj_kernel(x_ref, g_ref, w_ref, qg_ref, kg_ref, bf_ref, bd_ref, tri_ref,
                     q_ref, kv_ref, kvb_ref, logf_ref, c_ref, carry_ref, *, tiles_per_seq):
    i = pl.program_id(0)
    h = _rmsnorm_rows(x_ref[...], g_ref[...]).astype(BF16)
    bd = bd_ref[...]
    qg, kg = qg_ref[...], kg_ref[...]

    def chunk(c):
        return _dot(h, w_ref[:, c * LANE:(c + 1) * LANE])

    nch = HD // LANE
    for c in range(nch):
        y = _head_norm(chunk(c), qg, bd) * QK_SCALE
        q_ref[:, c * LANE:(c + 1) * LANE] = y.astype(q_ref.dtype)
    for c in range(nch):
        y = _head_norm(chunk(nch + c), kg, bd)
        kv_ref[:, c * LANE:(c + 1) * LANE] = y
        kvb_ref[:, c * LANE:(c + 1) * LANE] = y.astype(BF16)
    for c in range(nch, 2 * nch):
        y = chunk(nch + c)
        kv_ref[:, c * LANE:(c + 1) * LANE] = y
        kvb_ref[:, c * LANE:(c + 1) * LANE] = y.astype(BF16)
    logf = _log_sigmoid(chunk(3 * nch) + bf_ref[...])
    logf_ref[...] = logf

    @pl.when(i % tiles_per_seq == 0)
    def _():
        carry_ref[...] = jnp.zeros_like(carry_ref)

    c_tile = _dot_w3(tri_ref[...], logf) + carry_ref[...]
    c_ref[...] = c_tile
    carry_ref[...] = c_tile[c_tile.shape[0] - 1:, :]


def _sb_proj_kernel(x_ref, g_ref, w_ref, q_ref, kv_ref, kvb_ref):
    h = _rmsnorm_rows(x_ref[...], g_ref[...]).astype(BF16)
    nch = HD // LANE
    for c in range(nch):
        y = _dot(h, w_ref[:, c * LANE:(c + 1) * LANE]) * QK_SCALE
        q_ref[:, c * LANE:(c + 1) * LANE] = y.astype(q_ref.dtype)
    for c in range(nch, 3 * nch):
        y = _dot(h, w_ref[:, c * LANE:(c + 1) * LANE])
        kv_ref[:, (c - nch) * LANE:(c - nch + 1) * LANE] = y
        kvb_ref[:, (c - nch) * LANE:(c - nch + 1) * LANE] = y.astype(BF16)


def _pad_cols(w, n):
    return jnp.pad(w, ((0, 0), (0, n - w.shape[1])))


def _head_avg_matrix():
    r = np.arange(LANE)
    return jnp.asarray(((r[:, None] // HEAD_DIM) == (r[None, :] // HEAD_DIM)).astype(np.float32) / HEAD_DIM, BF16)


def _pair_gain(gain):
    return jnp.tile(gain.reshape(1, HEAD_DIM), (1, LANE // HEAD_DIM)).astype(F32)


def _rope_tables(pos):
    half = ROT_DIM // 2
    inv_freq = ROPE_THETA ** (-jnp.arange(half, dtype=F32) * (2.0 / ROT_DIM))
    ang = pos.astype(F32)[:, None] * inv_freq[None, :]
    cos, sin = jnp.cos(ang), jnp.sin(ang)
    t = pos.shape[0]
    one = jnp.ones((t, HEAD_DIM - ROT_DIM), F32)
    zero_h = jnp.zeros((t, half), F32)
    zero_r = jnp.zeros((t, HEAD_DIM - ROT_DIM), F32)
    cos_h = jnp.concatenate([cos, cos, one], axis=1)
    s_lo_h = jnp.concatenate([-sin, zero_h, zero_r], axis=1)
    s_hi_h = jnp.concatenate([zero_h, sin, zero_r], axis=1)
    rep = LANE // HEAD_DIM
    return jnp.tile(cos_h, (1, rep)), jnp.tile(s_lo_h, (1, rep)), jnp.tile(s_hi_h, (1, rep))


def _proj_tm(m):
    return 256 if m % 256 == 0 else m


def _nsa_project(x, g, w_in, q_gain, k_gain, pos, q_dtype):
    m, d = x.shape
    tm = _proj_tm(m)
    n_in = 21 * LANE
    w = _pad_cols(w_in, n_in).astype(BF16)
    cos, s_lo, s_hi = _rope_tables(pos)
    p = pos.shape[0]
    if p == 1:
        cos, s_lo, s_hi = (jnp.broadcast_to(t, (tm, LANE)) for t in (cos, s_lo, s_hi))
        tab_spec = _const_spec((tm, LANE))
    else:
        per = p // tm
        tab_spec = pl.BlockSpec((tm, LANE), lambda i: (i % per, 0))
    row = lambda n: pl.BlockSpec((tm, n), lambda i: (i, 0))
    outs = pl.pallas_call(
        _nsa_proj_kernel,
        grid=(m // tm,),
        in_specs=[row(d), _const_spec((1, d)), _const_spec((d, n_in)), _const_spec((1, LANE)), _const_spec((1, LANE)),
                  tab_spec, tab_spec, tab_spec, _const_spec((LANE, LANE))],
        out_specs=[row(HD), row(4 * NSA_KVD), row(2 * NSA_KVD), row(6 * NSA_KVD), row(LANE)],
        out_shape=[jax.ShapeDtypeStruct((m, HD), q_dtype),
                   jax.ShapeDtypeStruct((m, 4 * NSA_KVD), F32),
                   jax.ShapeDtypeStruct((m, 2 * NSA_KVD), F32),
                   jax.ShapeDtypeStruct((m, 6 * NSA_KVD), BF16),
                   jax.ShapeDtypeStruct((m, LANE), F32)],
        compiler_params=_cp(("parallel",)),
        name="nsa_proj",
    )(x, g.reshape(1, d), w, _pair_gain(q_gain), _pair_gain(k_gain), cos, s_lo, s_hi, _head_avg_matrix())
    return outs


def _fox_project(x, g, w_in, b_f, q_gain, k_gain, seq_len, q_dtype):
    m, d = x.shape
    tm = _proj_tm(m)
    n_in = 25 * LANE
    w = _pad_cols(w_in, n_in).astype(BF16)
    bf = jnp.pad(b_f.reshape(1, N_HEADS), ((0, 0), (0, LANE - N_HEADS))).astype(F32)
    tri = jnp.asarray(np.tril(np.ones((tm, tm), np.float32)), BF16)
    tiles_per_seq = max(seq_len // tm, 1)
    row = lambda n: pl.BlockSpec((tm, n), lambda i: (i, 0))
    outs = pl.pallas_call(
        functools.partial(_fox_proj_kernel, tiles_per_seq=tiles_per_seq),
        grid=(m // tm,),
        in_specs=[row(d), _const_spec((1, d)), _const_spec((d, n_in)), _const_spec((1, LANE)), _const_spec((1, LANE)),
                  _const_spec((1, LANE)), _const_spec((LANE, LANE)), _const_spec((tm, tm))],
        out_specs=[row(HD), row(2 * HD), row(2 * HD), row(LANE), row(LANE)],
        out_shape=[jax.ShapeDtypeStruct((m, HD), q_dtype),
                   jax.ShapeDtypeStruct((m, 2 * HD), F32),
                   jax.ShapeDtypeStruct((m, 2 * HD), BF16),
                   jax.ShapeDtypeStruct((m, LANE), F32),
                   jax.ShapeDtypeStruct((m, LANE), F32)],
        scratch_shapes=[pltpu.VMEM((1, LANE), F32)],
        compiler_params=_cp(("arbitrary",)),
        name="fox_proj",
    )(x, g.reshape(1, d), w, _pair_gain(q_gain), _pair_gain(k_gain), bf, _head_avg_matrix(), tri)
    return outs


def _sb_project(x, g, w_in, q_dtype):
    m, d = x.shape
    tm = _proj_tm(m)
    w = w_in.astype(BF16)
    row = lambda n: pl.BlockSpec((tm, n), lambda i: (i, 0))
    return pl.pallas_call(
        _sb_proj_kernel,
        grid=(m // tm,),
        in_specs=[row(d), _const_spec((1, d)), _const_spec(w.shape)],
        out_specs=[row(HD), row(2 * HD), row(2 * HD)],
        out_shape=[jax.ShapeDtypeStruct((m, HD), q_dtype),
                   jax.ShapeDtypeStruct((m, 2 * HD), F32),
                   jax.ShapeDtypeStruct((m, 2 * HD), BF16)],
        compiler_params=_cp(("parallel",)),
        name="sb_proj",
    )(x, g.reshape(1, d), w)


def _split_heads(q, lane):
    zero = jnp.zeros_like(q)
    return jnp.concatenate([jnp.where(lane < HEAD_DIM, q, zero), jnp.where(lane >= HEAD_DIM, q, zero)], axis=0)


def _fox_attn_kernel(q_ref, k_ref, v_ref, ct_ref, o_ref, qs_ref, m_ref, l_ref, acc_ref, *, t):
    i = pl.program_id(2)
    lane = lax.broadcasted_iota(jnp.int32, (t, LANE), 1)
    qs_ref[...] = _split_heads(q_ref[0], lane)
    m_ref[...] = jnp.full_like(m_ref, NEG_INF)
    l_ref[...] = jnp.zeros_like(l_ref)
    acc_ref[...] = jnp.zeros_like(acc_ref)
    row = lax.broadcasted_iota(jnp.int32, (t, t), 0) + i * t
    col = lax.broadcasted_iota(jnp.int32, (t, t), 1)

    def body(j, carry):
        off = pl.multiple_of(j * t, t)
        k = k_ref[0, pl.ds(off, t), :]
        v = v_ref[0, pl.ds(off, t), :]
        s = _nt_dot(qs_ref[...], k).reshape(2, t, t)
        ct = ct_ref[0, 0, :, pl.ds(off, t)]
        s = s - ct[:, None, :]
        valid = (col + off) <= row
        s = jnp.where(valid[None], s, NEG_INF).reshape(2 * t, t)
        m_old = m_ref[...]
        m_new = jnp.maximum(m_old, jnp.max(s, axis=-1, keepdims=True))
        p = jnp.exp(s - m_new)
        alpha = jnp.exp(m_old - m_new)
        l_ref[...] = alpha * l_ref[...] + jnp.sum(p, axis=-1, keepdims=True)
        acc_ref[...] = alpha * acc_ref[...] + _dot(p.astype(BF16), v)
        m_ref[...] = m_new
        return carry

    lax.fori_loop(0, i + 1, body, 0)
    o = acc_ref[...] / l_ref[...]
    o_ref[0] = jnp.where(lane < HEAD_DIM, o[:t], o[t:]).astype(o_ref.dtype)


def _fox_attention(q, kvb, c, n, s):
    t = min(256, s)
    npair = HD // LANE
    ct = c[:, :N_HEADS].reshape(n, s, npair, 2).transpose(0, 2, 3, 1)
    return pl.pallas_call(
        functools.partial(_fox_attn_kernel, t=t),
        grid=(n, npair, s // t),
        in_specs=[
            pl.BlockSpec((1, t, LANE), lambda b, p, i: (b, i, p)),
            pl.BlockSpec((1, s, LANE), lambda b, p, i: (b, 0, p)),
            pl.BlockSpec((1, s, LANE), lambda b, p, i: (b, 0, npair + p)),
            pl.BlockSpec((1, 1, 2, s), lambda b, p, i: (b, p, 0, 0)),
        ],
        out_specs=pl.BlockSpec((1, t, LANE), lambda b, p, i: (b, i, p)),
        out_shape=jax.ShapeDtypeStruct((n, s, HD), BF16),
        scratch_shapes=[pltpu.VMEM((2 * t, LANE), BF16), pltpu.VMEM((2 * t, 1), F32), pltpu.VMEM((2 * t, 1), F32),
                        pltpu.VMEM((2 * t, LANE), F32)],
        compiler_params=_cp(("parallel", "parallel", "arbitrary")),
        name="fox_attn",
    )(q.reshape(n, s, HD), kvb.reshape(n, s, 2 * HD), kvb.reshape(n, s, 2 * HD), ct).reshape(n * s, HD)


def _sb_attn_kernel(q_ref, k_ref, v_ref, u_ref, o_ref, qs_ref, r_ref, acc_ref, *, t):
    i = pl.program_id(2)
    lane = lax.broadcasted_iota(jnp.int32, (t, LANE), 1)
    qs_ref[...] = _split_heads(q_ref[0], lane)
    r_ref[...] = jnp.zeros_like(r_ref)
    acc_ref[...] = jnp.zeros_like(acc_ref)
    row = lax.broadcasted_iota(jnp.int32, (t, t), 0) + i * t
    col = lax.broadcasted_iota(jnp.int32, (t, t), 1)

    def body(jj, carry):
        off = pl.multiple_of((i - jj) * t, t)
        k = k_ref[0, pl.ds(off, t), :]
        v = v_ref[0, pl.ds(off, t), :]
        z = _nt_dot(qs_ref[...], k).reshape(2, t, t)
        valid = ((col + off) < row)[None]
        lg = jnp.log1p(jnp.exp(-jnp.abs(z)))
        ls = jnp.where(valid, -(jnp.maximum(z, 0.0) + lg), 0.0).reshape(2 * t, t)
        later = _dot_x2(ls, u_ref[...]) + r_ref[...]
        loga = (jnp.minimum(z, 0.0) - lg).reshape(2 * t, t) + later
        a = jnp.where(jnp.broadcast_to(valid, (2, t, t)).reshape(2 * t, t), jnp.exp(loga), 0.0)
        acc_ref[...] += _dot(a.astype(BF16), v)
        r_ref[...] += jnp.sum(ls, axis=-1, keepdims=True)
        return carry

    lax.fori_loop(0, i + 1, body, 0)
    o = acc_ref[...]
    o_ref[0] = jnp.where(lane < HEAD_DIM, o[:t], o[t:]).astype(o_ref.dtype)


def _suffix_matrix(t):
    r = np.arange(t)
    return jnp.asarray((r[:, None] > r[None, :]).astype(np.float32), BF16)


def _sb_attention(q, kvb, n, s):
    t = min(256, s)
    npair = HD // LANE
    return pl.pallas_call(
        functools.partial(_sb_attn_kernel, t=t),
        grid=(n, npair, s // t),
        in_specs=[
            pl.BlockSpec((1, t, LANE), lambda b, p, i: (b, i, p)),
            pl.BlockSpec((1, s, LANE), lambda b, p, i: (b, 0, p)),
            pl.BlockSpec((1, s, LANE), lambda b, p, i: (b, 0, npair + p)),
            _const_spec((t, t)),
        ],
        out_specs=pl.BlockSpec((1, t, LANE), lambda b, p, i: (b, i, p)),
        out_shape=jax.ShapeDtypeStruct((n, s, HD), BF16),
        scratch_shapes=[pltpu.VMEM((2 * t, LANE), BF16), pltpu.VMEM((2 * t, 1), F32), pltpu.VMEM((2 * t, LANE), F32)],
        compiler_params=_cp(("parallel", "parallel", "arbitrary")),
        name="sb_attn",
    )(q.reshape(n, s, HD), kvb.reshape(n, s, 2 * HD), kvb.reshape(n, s, 2 * HD), _suffix_matrix(t)).reshape(n * s, HD)


def _cmp_kernel(x_ref, pe_ref, w1_ref, w2_ref, o_ref):
    g, length, width = x_ref.shape
    nb = length // CMP_BLOCK

    def body(p, acc):
        x = x_ref[:, pl.ds(p, nb, stride=CMP_BLOCK), :] + pe_ref[0, pl.ds(p, 1), :]
        return acc + _dot(x.reshape(g * nb, width).astype(BF16), w1_ref[0, p])

    acc = lax.fori_loop(0, CMP_BLOCK, body, jnp.zeros((g * nb, width), F32))
    hid = (acc * jax.nn.sigmoid(acc)).astype(BF16)
    o_ref[:, 0] = _dot(hid, w2_ref[0]).reshape(g, nb, width).astype(o_ref.dtype)


def _cmp_weights(cmp_pos, cmp_w1, cmp_w2):
    rep = LANE // HEAD_DIM
    eye = jnp.eye(rep, dtype=F32)
    w1 = cmp_w1.reshape(2, CMP_BLOCK, HEAD_DIM, HEAD_DIM)
    w1bd = jnp.einsum('hg,cpde->cphdge', eye, w1).reshape(2, CMP_BLOCK, LANE, LANE).astype(BF16)
    w2bd = jnp.einsum('hg,cde->chdge', eye, cmp_w2).reshape(2, LANE, LANE).astype(BF16)
    pe = jnp.tile(cmp_pos, (1, 1, rep)).astype(F32)
    return pe, w1bd, w2bd


def _nsa_compress(rows, n, length, group, cmp_w, out_dtype):
    pe, w1bd, w2bd = cmp_w
    nb = length // CMP_BLOCK
    halves = NSA_KVD // LANE
    return pl.pallas_call(
        _cmp_kernel,
        grid=(n // group, 2, halves),
        in_specs=[
            pl.BlockSpec((group, length, LANE), lambda b, c, f: (b, 0, c * halves + f)),
            pl.BlockSpec((1, CMP_BLOCK, LANE), lambda b, c, f: (c, 0, 0)),
            pl.BlockSpec((1, CMP_BLOCK, LANE, LANE), lambda b, c, f: (c, 0, 0, 0)),
            pl.BlockSpec((1, LANE, LANE), lambda b, c, f: (c, 0, 0)),
        ],
        out_specs=pl.BlockSpec((group, 1, nb, LANE), lambda b, c, f: (b, c, 0, f)),
        out_shape=jax.ShapeDtypeStruct((n, 2, nb, NSA_KVD), out_dtype),
        compiler_params=_cp(("parallel", "arbitrary", "arbitrary")),
        name="nsa_compress",
    )(rows, pe, w1bd, w2bd)


def _masked_softmax_rows(s, mask):
    s = jnp.where(mask, s, NEG_INF)
    m = jnp.max(s, axis=-1, keepdims=True)
    e = jnp.where(mask, jnp.exp(s - m), 0.0)
    return e / jnp.maximum(jnp.sum(e, axis=-1, keepdims=True), 1e-30)


def _both_halves(x, lane, low):
    y = jnp.where((lane < HEAD_DIM) if low else (lane >= HEAD_DIM), x, 0.0)
    return y + pltpu.roll(y, HEAD_DIM, 1)


def _nsa_attn_kernel(q_ref, gate_ref, kc_ref, vc_ref, ks_ref, vs_ref, kw_ref, vw_ref, e_ref, o_ref,
                     q4_ref, sc_ref, m_ref, l_ref, acc_ref, *, tq, tk, wl):
    h = pl.program_id(1)
    i = pl.program_id(2)
    s0 = i * tq
    lane = lax.broadcasted_iota(jnp.int32, (tq, LANE), 1)
    mine = (lane // HEAD_DIM) == (h % 2)

    for c in range(2):
        qc = q_ref[0, :, c * LANE:(c + 1) * LANE].astype(F32)
        q4_ref[(2 * c) * tq:(2 * c + 1) * tq, :] = jnp.where(mine, _both_halves(qc, lane, True), 0.0).astype(BF16)
        q4_ref[(2 * c + 1) * tq:(2 * c + 2) * tq, :] = jnp.where(mine, _both_halves(qc, lane, False), 0.0).astype(BF16)
    q4 = q4_ref[...]

    kc = kc_ref[0, 0]
    nc = kc.shape[0]
    t_c = s0 + lax.broadcasted_iota(jnp.int32, (tq, nc), 0)
    b_c = lax.broadcasted_iota(jnp.int32, (tq, nc), 1)
    cm = ((b_c + 1) * CMP_BLOCK - 1 <= t_c)[None]
    p_c = _masked_softmax_rows(_nt_dot(q4, kc).reshape(4, tq, nc), cm)
    o_c = _dot(p_c.reshape(4 * tq, nc).astype(BF16), vc_ref[0, 0])

    blk = lax.broadcasted_iota(jnp.int32, (nc, tq), 0)
    t_t = s0 + lax.broadcasted_iota(jnp.int32, (nc, tq), 1)
    cm_t = (blk + 1) * CMP_BLOCK - 1 <= t_t
    cm4 = jnp.concatenate([cm_t] * 4, axis=1)
    s_t = jnp.where(cm4, _nt_dot(kc, q4), NEG_INF)
    e_t = jnp.where(cm4, jnp.exp(s_t - jnp.max(s_t, axis=0, keepdims=True)), 0.0)
    p_t = e_t / jnp.maximum(jnp.sum(e_t, axis=0, keepdims=True), 1e-30)
    imp = p_t[:, 0:tq] + p_t[:, tq:2 * tq] + p_t[:, 2 * tq:3 * tq] + p_t[:, 3 * tq:4 * tq]
    cur = t_t // SEL_BLOCK
    valid = blk <= cur
    forced = valid & ((blk == 0) | (blk == cur) | (blk == cur - 1))
    score = jnp.where(forced, FORCE_SCORE, jnp.where(valid, imp, NEG_INF))
    sc_ref[...] = score

    def rank_body(j, rank):
        r = sc_ref[pl.ds(j, 1), :]
        beats = (r > score) | ((r == score) & (j < blk))
        return rank + beats.astype(jnp.int32)

    n_blk = (s0 + tq - 1) // SEL_BLOCK + 1
    rank = lax.fori_loop(0, n_blk, rank_body, jnp.zeros((nc, tq), jnp.int32))
    sel = jnp.where((rank < TOP_N) & valid, 1.0, 0.0).T.astype(BF16)

    m_ref[...] = jnp.full_like(m_ref, NEG_INF)
    l_ref[...] = jnp.zeros_like(l_ref)
    acc_ref[...] = jnp.zeros_like(acc_ref)
    t_s = s0 + lax.broadcasted_iota(jnp.int32, (tq, tk), 0)
    k_s = lax.broadcasted_iota(jnp.int32, (tq, tk), 1)

    def slc_body(kt, carry):
        off = pl.multiple_of(kt * tk, tk)
        k = ks_ref[0, pl.ds(off, tk), :]
        v = vs_ref[0, pl.ds(off, tk), :]
        in_sel = _dot(sel, e_ref[:, pl.ds(off, tk)])
        ok = (in_sel > 0.5) & (k_s + off <= t_s)
        ok4 = jnp.broadcast_to(ok[None], (4, tq, tk)).reshape(4 * tq, tk)
        s = jnp.where(ok4, _nt_dot(q4, k), NEG_INF)
        m_old = m_ref[...]
        m_new = jnp.maximum(m_old, jnp.max(s, axis=-1, keepdims=True))
        p = jnp.where(ok4, jnp.exp(s - m_new), 0.0)
        alpha = jnp.exp(m_old - m_new)
        l_ref[...] = alpha * l_ref[...] + jnp.sum(p, axis=-1, keepdims=True)
        acc_ref[...] = alpha * acc_ref[...] + _dot(p.astype(BF16), v)
        m_ref[...] = m_new
        return carry

    lax.fori_loop(0, (s0 + tq + tk - 1) // tk, slc_body, 0)
    o_s = acc_ref[...] / jnp.maximum(l_ref[...], 1e-30)

    st = pl.multiple_of(jnp.maximum(s0 - WINDOW, 0), tq)
    t_w = s0 + lax.broadcasted_iota(jnp.int32, (tq, wl), 0)
    k_w = st + lax.broadcasted_iota(jnp.int32, (tq, wl), 1)
    dist = t_w - k_w
    wm = ((dist >= 0) & (dist <= WINDOW))[None]
    p_w = _masked_softmax_rows(_nt_dot(q4, kw_ref[0, pl.ds(st, wl), :]).reshape(4, tq, wl), wm)
    o_w = _dot(p_w.reshape(4 * tq, wl).astype(BF16), vw_ref[0, pl.ds(st, wl), :])

    n_gate = NSA_GROUP * 3
    r_i = lax.broadcasted_iota(jnp.int32, (LANE, n_gate * LANE), 0)
    l_i = lax.broadcasted_iota(jnp.int32, (LANE, n_gate * LANE), 1)
    onehot = jnp.where(r_i == n_gate * h + l_i // LANE, 1.0, 0.0).astype(BF16)
    gx = _dot_x2(gate_ref[0], onehot)

    outs = []
    for g in range(NSA_GROUP):
        rs = slice(g * tq, (g + 1) * tq)
        gc, gs, gw = (gx[:, (3 * g + b) * LANE:(3 * g + b + 1) * LANE] for b in range(3))
        og = jnp.where(mine, gc * o_c[rs] + gs * o_s[rs] + gw * o_w[rs], 0.0)
        outs.append(og + pltpu.roll(og, HEAD_DIM, 1))
    for c in range(2):
        o_ref[0, :, c * LANE:(c + 1) * LANE] = jnp.where(lane < HEAD_DIM, outs[2 * c], outs[2 * c + 1]).astype(o_ref.dtype)


def _block_expand_matrix(nb, length):
    return jnp.asarray((np.arange(nb)[:, None] == (np.arange(length)[None, :] // SEL_BLOCK)).astype(np.float32), BF16)


def _nsa_attention(q, gates, kvb, cmp_kv, n, s):
    tq = 128
    tk = min(512, s)
    wl = min(WINDOW + tq, s)
    nc = s // CMP_BLOCK
    kv_spec = lambda chunk0: pl.BlockSpec((1, s, LANE), lambda b, h, i: (b, 0, chunk0 + h // 2))
    kvb3 = kvb.reshape(n, s, 6 * NSA_KVD)
    return pl.pallas_call(
        functools.partial(_nsa_attn_kernel, tq=tq, tk=tk, wl=wl),
        grid=(n, NSA_KV_HEADS, s // tq),
        in_specs=[
            pl.BlockSpec((1, tq, NSA_GROUP * HEAD_DIM), lambda b, h, i: (b, i, h)),
            pl.BlockSpec((1, tq, LANE), lambda b, h, i: (b, i, 0)),
            pl.BlockSpec((1, 1, nc, LANE), lambda b, h, i: (b, 0, 0, h // 2)),
            pl.BlockSpec((1, 1, nc, LANE), lambda b, h, i: (b, 1, 0, h // 2)),
            kv_spec(4), kv_spec(6), kv_spec(8), kv_spec(10),
            _const_spec((nc, s)),
        ],
        out_specs=pl.BlockSpec((1, tq, NSA_GROUP * HEAD_DIM), lambda b, h, i: (b, i, h)),
        out_shape=jax.ShapeDtypeStruct((n, s, HD), BF16),
        scratch_shapes=[pltpu.VMEM((4 * tq, LANE), BF16), pltpu.VMEM((nc, tq), F32), pltpu.VMEM((4 * tq, 1), F32),
                        pltpu.VMEM((4 * tq, 1), F32), pltpu.VMEM((4 * tq, LANE), F32)],
        compiler_params=_cp(("parallel", "parallel", "arbitrary")),
        name="nsa_attn",
    )(q.reshape(n, s, HD), gates.reshape(n, s, LANE), cmp_kv, cmp_kv, kvb3, kvb3, kvb3, kvb3,
      _block_expand_matrix(nc, s)).reshape(n * s, HD)


def _gather_kernel(pt_ref, cache_ref, out_ref, sem, *, layer, n_pages):
    b = pl.program_id(0)
    copies = [pltpu.make_async_copy(cache_ref.at[layer, pt_ref[b, p]], out_ref.at[b, p], sem.at[p])
              for p in range(n_pages)]
    for c in copies:
        c.start()
    for c in copies:
        c.wait()


def _gather_pages(cache, page_table, layer):
    nb, n_pages = page_table.shape
    page = cache.shape[2]
    width = int(np.prod(cache.shape[3:]))
    flat = cache.reshape(cache.shape[0], cache.shape[1], page, width)
    out = pl.pallas_call(
        functools.partial(_gather_kernel, layer=layer, n_pages=n_pages),
        grid_spec=pltpu.PrefetchScalarGridSpec(
            num_scalar_prefetch=1,
            grid=(nb,),
            in_specs=[pl.BlockSpec(memory_space=pl.ANY)],
            out_specs=pl.BlockSpec(memory_space=pl.ANY),
            scratch_shapes=[pltpu.SemaphoreType.DMA((n_pages,))],
        ),
        out_shape=jax.ShapeDtypeStruct((nb, n_pages, page, width), cache.dtype),
        compiler_params=pltpu.CompilerParams(dimension_semantics=("arbitrary",)),
        name="gather_pages",
    )(page_table, flat)
    return out.reshape(nb, n_pages * page, width)


def _head_segments(width, n_cols, col_of_head):
    seg = np.zeros((width, n_cols), np.float32)
    for hh in range(width // HEAD_DIM):
        seg[hh * HEAD_DIM:(hh + 1) * HEAD_DIM, col_of_head(hh)] = 1.0
    return seg


def _row8(x):
    return jnp.broadcast_to(x, (8, x.shape[1]))


def _row_expand(x, segt):
    return _dot_x3(_row8(x), segt)[0:1]


def _fox_dec_kernel(q_ref, kvn_ref, lfn_ref, k_ref, v_ref, lf_ref, seg_ref, segt_ref, u_ref, o_ref,
                    m_ref, l_ref, acc_ref, *, r):
    ci = pl.program_id(1)
    n_chunk = pl.num_programs(1)

    @pl.when(ci == 0)
    def _():
        m_ref[...] = jnp.full_like(m_ref, NEG_INF)
        l_ref[...] = jnp.zeros_like(l_ref)
        acc_ref[...] = jnp.zeros_like(acc_ref)

    q = q_ref[0]
    seg, segt = seg_ref[...], segt_ref[...]
    lf_all = lf_ref[0]
    rows = lax.broadcasted_iota(jnp.int32, lf_all.shape, 0)
    lo = pl.multiple_of(ci * r, r)
    tail = jnp.sum(jnp.where(rows >= lo + r, lf_all, 0.0), axis=0, keepdims=True) + lfn_ref[0]
    w = _dot_w3(u_ref[...], lf_ref[0, pl.ds(lo, r), :]) + tail
    s = _dot((k_ref[0] * q).astype(BF16), seg) + w
    m_old = m_ref[...]
    m_new = jnp.maximum(m_old, jnp.max(s, axis=0, keepdims=True))
    p = jnp.exp(s - m_new)
    alpha = jnp.exp(m_old - m_new)
    l_ref[...] = alpha * l_ref[...] + jnp.sum(p, axis=0, keepdims=True)
    acc_ref[...] = _row_expand(alpha, segt) * acc_ref[...] + jnp.sum(_dot(p.astype(BF16), segt) * v_ref[0], axis=0, keepdims=True)
    m_ref[...] = m_new

    @pl.when(ci == n_chunk - 1)
    def _():
        kn = kvn_ref[0, :, 0:HD]
        vn = kvn_ref[0, :, HD:2 * HD]
        s_n = _dot(_row8((kn * q).astype(BF16)), seg)[0:1]
        m_f = jnp.maximum(m_ref[...], s_n)
        a_f = jnp.exp(m_ref[...] - m_f)
        p_n = jnp.exp(s_n - m_f)
        l_f = a_f * l_ref[...] + p_n
        acc = _row_expand(a_f, segt) * acc_ref[...] + _row_expand(p_n, segt) * vn
        o_ref[0] = acc * _row_expand(1.0 / l_f, segt)


def _strict_upper(r):
    i = np.arange(r)
    return jnp.asarray((i[None, :] > i[:, None]).astype(np.float32), BF16)


def _fox_decode(q, kv_new, logf_new, kv_past, lf_past):
    nb, length, _ = kv_past.shape
    r = min(1024, length)
    seg = _head_segments(HD, N_HEADS, lambda hh: hh)
    bspec = lambda shape, im: pl.BlockSpec(shape, im)
    return pl.pallas_call(
        functools.partial(_fox_dec_kernel, r=r),
        grid=(nb, length // r),
        in_specs=[
            bspec((1, 1, HD), lambda b, c: (b, 0, 0)),
            bspec((1, 1, 2 * HD), lambda b, c: (b, 0, 0)),
            bspec((1, 1, N_HEADS), lambda b, c: (b, 0, 0)),
            bspec((1, r, HD), lambda b, c: (b, c, 0)),
            bspec((1, r, HD), lambda b, c: (b, c, 1)),
            bspec((1, length, N_HEADS), lambda b, c: (b, 0, 0)),
            _const_spec((HD, N_HEADS)), _const_spec((N_HEADS, HD)), _const_spec((r, r)),
        ],
        out_specs=bspec((1, 1, HD), lambda b, c: (b, 0, 0)),
        out_shape=jax.ShapeDtypeStruct((nb, 1, HD), F32),
        scratch_shapes=[pltpu.VMEM((1, N_HEADS), F32), pltpu.VMEM((1, N_HEADS), F32), pltpu.VMEM((1, HD), F32)],
        compiler_params=_cp(("parallel", "arbitrary")),
        name="fox_decode",
    )(q.reshape(nb, 1, HD), kv_new.reshape(nb, 1, 2 * HD), logf_new.reshape(nb, 1, N_HEADS), kv_past, kv_past, lf_past,
      jnp.asarray(seg, BF16), jnp.asarray(seg.T, BF16), _strict_upper(r)).reshape(nb, HD)


def _sb_dec_kernel(q_ref, k_ref, v_ref, seg_ref, segt_ref, u_ref, o_ref, r_ref, acc_ref):
    ci = pl.program_id(1)

    @pl.when(ci == 0)
    def _():
        r_ref[...] = jnp.zeros_like(r_ref)
        acc_ref[...] = jnp.zeros_like(acc_ref)

    seg, segt = seg_ref[...], segt_ref[...]
    z = _dot((k_ref[0] * q_ref[0]).astype(BF16), seg)
    lg = jnp.log1p(jnp.exp(-jnp.abs(z)))
    ls = -(jnp.maximum(z, 0.0) + lg)
    later = _dot_w3(u_ref[...], ls) + r_ref[...]
    a = jnp.exp(jnp.minimum(z, 0.0) - lg + later)
    acc_ref[...] += jnp.sum(_dot(a.astype(BF16), segt) * v_ref[0], axis=0, keepdims=True)
    r_ref[...] += jnp.sum(ls, axis=0, keepdims=True)
    o_ref[0] = acc_ref[...]


def _sb_decode(q, kv_past):
    nb, length, _ = kv_past.shape
    r = min(1024, length)
    n_chunk = length // r
    seg = _head_segments(HD, N_HEADS, lambda hh: hh)
    return pl.pallas_call(
        _sb_dec_kernel,
        grid=(nb, n_chunk),
        in_specs=[
            pl.BlockSpec((1, 1, HD), lambda b, c: (b, 0, 0)),
            pl.BlockSpec((1, r, HD), lambda b, c: (b, n_chunk - 1 - c, 0)),
            pl.BlockSpec((1, r, HD), lambda b, c: (b, n_chunk - 1 - c, 1)),
            _const_spec((HD, N_HEADS)), _const_spec((N_HEADS, HD)), _const_spec((r, r)),
        ],
        out_specs=pl.BlockSpec((1, 1, HD), lambda b, c: (b, 0, 0)),
        out_shape=jax.ShapeDtypeStruct((nb, 1, HD), F32),
        scratch_shapes=[pltpu.VMEM((1, N_HEADS), F32), pltpu.VMEM((1, HD), F32)],
        compiler_params=_cp(("parallel", "arbitrary")),
        name="sb_decode",
    )(q.reshape(nb, 1, HD), kv_past, kv_past, jnp.asarray(seg, BF16), jnp.asarray(seg.T, BF16),
      _strict_upper(r)).reshape(nb, HD)


def _dec_branch(k, v, qg, segs, segts, mask, k_new, v_new):
    s = None
    for g in range(NSA_GROUP):
        part = _dot((k * qg[g]).astype(BF16), segs[g])
        s = part if s is None else s + part
    if mask is not None:
        s = jnp.where(mask, s, NEG_INF)
    m = jnp.max(s, axis=0, keepdims=True)
    if k_new is not None:
        s_n = None
        for g in range(NSA_GROUP):
            part = _dot(_row8((k_new * qg[g]).astype(BF16)), segs[g])[0:1]
            s_n = part if s_n is None else s_n + part
        m = jnp.maximum(m, s_n)
    e = jnp.exp(s - m)
    if mask is not None:
        e = jnp.where(mask, e, 0.0)
    l = jnp.sum(e, axis=0, keepdims=True)
    if k_new is not None:
        e_n = jnp.exp(s_n - m)
        l = l + e_n
    inv = 1.0 / l
    eb = e.astype(BF16)
    outs = []
    for g in range(NSA_GROUP):
        o = jnp.sum(_dot(eb, segts[g]) * v, axis=0, keepdims=True)
        if k_new is not None:
            o = o + _row_expand(e_n, segts[g]) * v_new
        outs.append(o * _row_expand(inv, segts[g]))
    return outs, e * inv


def _nsa_dec_kernel(q_ref, gx_ref, new_ref, wnew_ref, cmp_ref, ks_ref, vs_ref, win_ref,
                    seg_ref, segt_ref, rep_ref, erep_ref, o_ref, *, n_blk):
    segs = [seg_ref[g] for g in range(NSA_GROUP)]
    segts = [segt_ref[g] for g in range(NSA_GROUP)]
    qg = [q_ref[0, :, g * NSA_KVD:(g + 1) * NSA_KVD] for g in range(NSA_GROUP)]
    ncol = NSA_GROUP * NSA_KV_HEADS

    o_c, p_c = _dec_branch(cmp_ref[0, 0], cmp_ref[0, 1], qg, segs, segts, None, None, None)

    imp = _dot_x3(p_c, rep_ref[...])
    nc = imp.shape[0]
    pad = erep_ref.shape[1] - nc
    imp = jnp.concatenate([imp, jnp.zeros((pad, ncol), F32)], axis=0)
    blk = lax.broadcasted_iota(jnp.int32, imp.shape, 0)
    cur = n_blk - 1
    valid = blk <= cur
    forced = (blk == 0) | (blk == cur) | (blk == cur - 1)
    score = jnp.where(forced, FORCE_SCORE, jnp.where(valid, imp, NEG_INF))
    rank = jnp.zeros(imp.shape, jnp.int32)
    for j in range(n_blk):
        rj = score[j:j + 1, :]
        rank = rank + ((rj > score) | ((rj == score) & (j < blk))).astype(jnp.int32)
    sel = jnp.where((rank < TOP_N) & valid, 1.0, 0.0).astype(BF16)

    in_sel = _dot(erep_ref[...], sel) > 0.5
    k_new = new_ref[0, :, 2 * NSA_KVD:3 * NSA_KVD]
    v_new = new_ref[0, :, 3 * NSA_KVD:4 * NSA_KVD]
    o_s, _ = _dec_branch(ks_ref[0], vs_ref[0], qg, segs, segts, in_sel, k_new, v_new)

    o_w, _ = _dec_branch(win_ref[0, :, 0:NSA_KVD], win_ref[0, :, NSA_KVD:2 * NSA_KVD], qg, segs, segts, None,
                         wnew_ref[0, :, 0:NSA_KVD], wnew_ref[0, :, NSA_KVD:2 * NSA_KVD])

    for g in range(NSA_GROUP):
        sl = slice(g * NSA_KVD, (g + 1) * NSA_KVD)
        o_ref[0, :, sl] = gx_ref[0, 0:1, sl] * o_c[g] + gx_ref[0, 1:2, sl] * o_s[g] + gx_ref[0, 2:3, sl] * o_w[g]


def _group_major(x):
    lead = x.shape[:-1]
    y = x.reshape(lead + (NSA_KV_HEADS, NSA_GROUP, HEAD_DIM))
    return jnp.swapaxes(y, -3, -2).reshape(lead + (HD,))


def _nsa_decode(q, gates, rows_new, win_new, cmp_kv, rows_past, win_buf):
    nb, length, _ = rows_past.shape
    wlen = win_buf.shape[1]
    nc = length // CMP_BLOCK
    n_blk = -(-(length + 1) // SEL_BLOCK)
    n_blk_pad = -(-n_blk // 8) * 8
    ncol = NSA_GROUP * NSA_KV_HEADS
    segs = np.stack([_head_segments(NSA_KVD, ncol, lambda hh, g=g: NSA_KV_HEADS * g + hh) for g in range(NSA_GROUP)])
    rep = (np.arange(ncol)[:, None] % NSA_KV_HEADS == np.arange(ncol)[None, :] % NSA_KV_HEADS).astype(np.float32)
    erep = (np.arange(length)[:, None] // SEL_BLOCK == np.arange(n_blk_pad)[None, :]).astype(np.float32)
    qg = _group_major(q)
    gx = gates[:, :N_HEADS * 3].reshape(nb, N_HEADS, 3)
    gx = jnp.broadcast_to(jnp.swapaxes(gx, 1, 2)[..., None], (nb, 3, N_HEADS, HEAD_DIM))
    gx = _group_major(gx.reshape(nb, 3, HD))
    out = pl.pallas_call(
        functools.partial(_nsa_dec_kernel, n_blk=n_blk),
        grid=(nb,),
        in_specs=[
            pl.BlockSpec((1, 1, HD), lambda b: (b, 0, 0)),
            pl.BlockSpec((1, 3, HD), lambda b: (b, 0, 0)),
            pl.BlockSpec((1, 1, 4 * NSA_KVD), lambda b: (b, 0, 0)),
            pl.BlockSpec((1, 1, 2 * NSA_KVD), lambda b: (b, 0, 0)),
            pl.BlockSpec((1, 2, nc, NSA_KVD), lambda b: (b, 0, 0, 0)),
            pl.BlockSpec((1, length, NSA_KVD), lambda b: (b, 0, 2)),
            pl.BlockSpec((1, length, NSA_KVD), lambda b: (b, 0, 3)),
            pl.BlockSpec((1, wlen, 2 * NSA_KVD), lambda b: (b, 0, 0)),
            _const_spec((NSA_GROUP, NSA_KVD, ncol)), _const_spec((NSA_GROUP, ncol, NSA_KVD)),
            _const_spec((ncol, ncol)), _const_spec((length, n_blk_pad)),
        ],
        out_specs=pl.BlockSpec((1, 1, HD), lambda b: (b, 0, 0)),
        out_shape=jax.ShapeDtypeStruct((nb, 1, HD), F32),
        compiler_params=_cp(("parallel",)),
        name="nsa_decode",
    )(qg.reshape(nb, 1, HD), gx, rows_new.reshape(nb, 1, 4 * NSA_KVD), win_new.reshape(nb, 1, 2 * NSA_KVD), cmp_kv,
      rows_past, rows_past, win_buf, jnp.asarray(segs, BF16), jnp.asarray(segs.transpose(0, 2, 1), BF16),
      jnp.asarray(rep, BF16), jnp.asarray(erep, BF16))
    o = out.reshape(nb, NSA_GROUP, NSA_KV_HEADS, HEAD_DIM)
    return jnp.swapaxes(o, 1, 2).reshape(nb, HD)


def kernel(x_prompt, x_sample, cache_nsa_kv, state_nsa_win, cache_fox_kv, cache_fox_logf, cache_sb_kv, page_table,
           ffn_norm, ffn_w_gate, ffn_w_up, ffn_w_down, mix_norm, nsa_w_in, nsa_q_gain, nsa_k_gain, nsa_cmp_pos,
           nsa_cmp_w1, nsa_cmp_w2, nsa_w_out, fox_w_in, fox_b_f, fox_q_gain, fox_k_gain, fox_w_out, sb_w_in, sb_w_out):
    n, s, d = x_prompt.shape
    nb, dec_seq, _ = x_sample.shape
    assert dec_seq == 1 and d == D_MODEL
    depth = ffn_norm.shape[0]
    page = cache_nsa_kv.shape[2]
    past = page_table.shape[1] * page
    xp = x_prompt.reshape(n * s, d)
    xs = x_sample.reshape(nb, d)
    wg, wu, wd = ffn_w_gate.astype(BF16), ffn_w_up.astype(BF16), ffn_w_down.astype(BF16)
    pos_p = jnp.arange(s, dtype=jnp.int32)
    pos_s = jnp.full((1,), past, jnp.int32)
    out = {k: [] for k in ("nsa_kv_p", "nsa_win_p", "fox_kv_p", "fox_lf_p", "sb_kv_p",
                           "nsa_kv_s", "nsa_win_s", "fox_kv_s", "fox_lf_s", "sb_kv_s")}
    for i in range(depth):
        j = i // 3
        xp = _ffn_half(xp, ffn_norm[i, 0], wg[i, 0], wu[i, 0], wd[i, 0])
        xs = _ffn_half(xs, ffn_norm[i, 0], wg[i, 0], wu[i, 0], wd[i, 0])
        if i % 3 == 0:
            qg, kg = nsa_q_gain[j], nsa_k_gain[j]
            w_out = nsa_w_out[j].astype(BF16)
            cmp_w = _cmp_weights(nsa_cmp_pos[j], nsa_cmp_w1[j], nsa_cmp_w2[j])
            q, rows, win, kvb, gates = _nsa_project(xp, mix_norm[i], nsa_w_in[j], qg, kg, pos_p, BF16)
            cmp_kv = _nsa_compress(rows.reshape(n, s, 4 * NSA_KVD), n, s, 1, cmp_w, BF16)
            xp = _out_proj(xp, _nsa_attention(q, gates, kvb, cmp_kv, n, s), w_out)
            out["nsa_kv_p"].append(rows.reshape(n, s, 4, NSA_KV_HEADS, HEAD_DIM))
            out["nsa_win_p"].append(win.reshape(n, s, 2, NSA_KV_HEADS, HEAD_DIM)[:, s - min(WINDOW, s):])

            q, rows, win, _, gates = _nsa_project(xs, mix_norm[i], nsa_w_in[j], qg, kg, pos_s, F32)
            rows_past = _gather_pages(cache_nsa_kv, page_table, j)
            cmp_kv = _nsa_compress(rows_past, nb, past, 4 if nb % 4 == 0 else 1, cmp_w, F32)
            wlen = state_nsa_win.shape[2]
            win_buf = state_nsa_win[j].reshape(nb, wlen, 2 * NSA_KVD)
            xs = _out_proj(xs, _nsa_decode(q, gates, rows, win, cmp_kv, rows_past, win_buf), w_out)
            out["nsa_kv_s"].append(rows.reshape(nb, 1, 4, NSA_KV_HEADS, HEAD_DIM))
            win_all = jnp.concatenate([win_buf, win.reshape(nb, 1, 2 * NSA_KVD)], axis=1)[:, 1:]
            out["nsa_win_s"].append(win_all.reshape(nb, wlen, 2, NSA_KV_HEADS, HEAD_DIM))
        elif i % 3 == 1:
            qg, kg = fox_q_gain[j], fox_k_gain[j]
            w_out = fox_w_out[j].astype(BF16)
            q, kv, kvb, logf, c = _fox_project(xp, mix_norm[i], fox_w_in[j], fox_b_f[j], qg, kg, s, BF16)
            xp = _out_proj(xp, _fox_attention(q, kvb, c, n, s), w_out)
            out["fox_kv_p"].append(kv.reshape(n, s, 2, N_HEADS, HEAD_DIM))
            out["fox_lf_p"].append(logf[:, :N_HEADS].reshape(n, s, N_HEADS))

            q, kv, _, logf, _ = _fox_project(xs, mix_norm[i], fox_w_in[j], fox_b_f[j], qg, kg, 1, F32)
            kv_past = _gather_pages(cache_fox_kv, page_table, j)
            lf_past = _gather_pages(cache_fox_logf, page_table, j)
            xs = _out_proj(xs, _fox_decode(q, kv, logf[:, :N_HEADS], kv_past, lf_past), w_out)
            out["fox_kv_s"].append(kv.reshape(nb, 1, 2, N_HEADS, HEAD_DIM))
            out["fox_lf_s"].append(logf[:, :N_HEADS].reshape(nb, 1, N_HEADS))
        else:
            w_out = sb_w_out[j].astype(BF16)
            q, kv, kvb = _sb_project(xp, mix_norm[i], sb_w_in[j], BF16)
            xp = _out_proj(xp, _sb_attention(q, kvb, n, s), w_out)
            out["sb_kv_p"].append(kv.reshape(n, s, 2, N_HEADS, HEAD_DIM))

            q, kv, _ = _sb_project(xs, mix_norm[i], sb_w_in[j], F32)
            kv_past = _gather_pages(cache_sb_kv, page_table, j)
            xs = _out_proj(xs, _sb_decode(q, kv_past), w_out)
            out["sb_kv_s"].append(kv.reshape(nb, 1, 2, N_HEADS, HEAD_DIM))
        xp = _ffn_half(xp, ffn_norm[i, 1], wg[i, 1], wu[i, 1], wd[i, 1])
        xs = _ffn_half(xs, ffn_norm[i, 1], wg[i, 1], wu[i, 1], wd[i, 1])
    st = {k: jnp.stack(v) for k, v in out.items()}
    return (xp.reshape(n, s, d), xs.reshape(nb, 1, d),
            st["nsa_kv_p"], st["nsa_win_p"], st["fox_kv_p"], st["fox_lf_p"], st["sb_kv_p"],
            st["nsa_kv_s"], st["nsa_win_s"], st["fox_kv_s"], st["fox_lf_s"], st["sb_kv_s"])
```

```python
import functools

import numpy as np
import jax
import jax.numpy as jnp
from jax import lax
from jax.experimental import pallas as pl
from jax.experimental.pallas import tpu as pltpu

F32 = jnp.float32
BF16 = jnp.bfloat16

D_MODEL = 1024
HEAD_DIM = 64
N_HEADS = 16
NSA_KV_HEADS = 4
NSA_GROUP = 4
ROT_DIM = 16
ROPE_THETA = 500000.0
CMP_BLOCK = 64
SEL_BLOCK = 64
TOP_N = 16
WINDOW = 512
NORM_EPS = 1e-6
FORCE_SCORE = 1e4
NEG_INF = -1e30
HD = N_HEADS * HEAD_DIM
NSA_KVD = NSA_KV_HEADS * HEAD_DIM
QK_SCALE = HEAD_DIM ** -0.5
F32_EXP_ZERO = -104.0

LANE = 128
V7X_VMEM_BYTES = 64 * 1024 * 1024
VMEM_LIMIT = 56 * 1024 * 1024


def _cp(sem, vmem=VMEM_LIMIT):
    return pltpu.CompilerParams(dimension_semantics=sem, vmem_limit_bytes=vmem)


def _dot(a, b):
    return jnp.dot(a, b, preferred_element_type=F32)


def _nt_dot(a, b):
    return lax.dot_general(a, b, (((1,), (1,)), ((), ())), preferred_element_type=F32)


def _dot_x2(x, w):
    hi = x.astype(BF16)
    lo = (x - hi.astype(F32)).astype(BF16)
    return _dot(hi, w) + _dot(lo, w)


def _dot_x3(x, w):
    hi = x.astype(BF16)
    r = x - hi.astype(F32)
    mid = r.astype(BF16)
    lo = (r - mid.astype(F32)).astype(BF16)
    return _dot(hi, w) + _dot(mid, w) + _dot(lo, w)


def _dot_w3(w, x):
    hi = x.astype(BF16)
    r = x - hi.astype(F32)
    mid = r.astype(BF16)
    lo = (r - mid.astype(F32)).astype(BF16)
    return _dot(w, hi) + _dot(w, mid) + _dot(w, lo)


def _rmsnorm_rows(x, g):
    ms = jnp.mean(x * x, axis=-1, keepdims=True)
    return x * lax.rsqrt(ms + NORM_EPS) * g


def _log_sigmoid(x):
    return jnp.minimum(x, 0.0) - jnp.log1p(jnp.exp(-jnp.abs(x)))


def _const_spec(shape):
    nd = len(shape)
    return pl.BlockSpec(shape, lambda *_: (0,) * nd)


def _ffn_kernel(x_ref, g_ref, wg_ref, wu_ref, wd_ref, o_ref, *, ff_chunk):
    x = x_ref[...]
    h = _rmsnorm_rows(x, g_ref[...]).astype(BF16)
    n_ff = wg_ref.shape[1]
    acc = None
    for c in range(n_ff // ff_chunk):
        sl = slice(c * ff_chunk, (c + 1) * ff_chunk)
        a = _dot(h, wg_ref[:, sl])
        b = _dot(h, wu_ref[:, sl])
        u = (a * jax.nn.sigmoid(a) * b).astype(BF16)
        part = _dot(u, wd_ref[sl, :])
        acc = part if acc is None else acc + part
    o_ref[...] = x + 0.5 * acc


def _ffn_half(x, g, wg, wu, wd):
    m, d = x.shape
    ff = wg.shape[1]
    tm = 256 if m % 256 == 0 else m
    ff_chunk = ff // 2 if (ff // 2) % LANE == 0 else ff
    return pl.pallas_call(
        functools.partial(_ffn_kernel, ff_chunk=ff_chunk),
        grid=(m // tm,),
        in_specs=[
            pl.BlockSpec((tm, d), lambda i: (i, 0)),
            _const_spec((1, d)),
            _const_spec((d, ff)),
            _const_spec((d, ff)),
            _const_spec((ff, d)),
        ],
        out_specs=pl.BlockSpec((tm, d), lambda i: (i, 0)),
        out_shape=jax.ShapeDtypeStruct((m, d), F32),
        compiler_params=_cp(("parallel",)),
        name="ffn_half",
    )(x, g.reshape(1, d), wg, wu, wd)


def _outproj_kernel(x_ref, o_ref, w_ref, y_ref):
    y_ref[...] = x_ref[...] + _dot(o_ref[...].astype(BF16), w_ref[...])


def _out_proj(x, o, w):
    m, d = x.shape
    tm = 512 if m % 512 == 0 else m
    return pl.pallas_call(
        _outproj_kernel,
        grid=(m // tm,),
        in_specs=[
            pl.BlockSpec((tm, d), lambda i: (i, 0)),
            pl.BlockSpec((tm, o.shape[1]), lambda i: (i, 0)),
            _const_spec(w.shape),
        ],
        out_specs=pl.BlockSpec((tm, d), lambda i: (i, 0)),
        out_shape=jax.ShapeDtypeStruct((m, d), F32),
        compiler_params=_cp(("parallel",)),
        name="out_proj",
    )(x, o, w)


def _head_norm(x, gain, bd):
    ms = _dot_x2(x * x, bd)
    return x * lax.rsqrt(ms + NORM_EPS) * gain


def _rope(y, cos, s_lo, s_hi):
    return y * cos + pltpu.roll(y, 8, 1) * s_hi + pltpu.roll(y, LANE - 8, 1) * s_lo


def _nsa_proj_kernel(x_ref, g_ref, w_ref, qg_ref, kg_ref, cos_ref, slo_ref, shi_ref, bd_ref,
                     q_ref, rows_ref, win_ref, kvb_ref, gate_ref):
    h = _rmsnorm_rows(x_ref[...], g_ref[...]).astype(BF16)
    cos, s_lo, s_hi = cos_ref[...], slo_ref[...], shi_ref[...]
    bd = bd_ref[...]
    qg, kg = qg_ref[...], kg_ref[...]

    def chunk(c):
        return _dot(h, w_ref[:, c * LANE:(c + 1) * LANE])

    for c in range(HD // LANE):
        y = _rope(_head_norm(chunk(c), qg, bd), cos, s_lo, s_hi) * QK_SCALE
        q_ref[:, c * LANE:(c + 1) * LANE] = y.astype(q_ref.dtype)
    base = HD // LANE
    for b in range(3):
        for part in range(2):
            for cc in range(2):
                c = base + 4 * b + 2 * part + cc
                y = chunk(c)
                if part == 0:
                    y = _rope(_head_norm(y, kg, bd), cos, s_lo, s_hi)
                col = (2 * part + cc) * LANE
                if b < 2:
                    rows_ref[:, 2 * b * NSA_KVD + col - 0:2 * b * NSA_KVD + col + LANE] = y
                else:
                    win_ref[:, col:col + LANE] = y
                kcol = (4 * b + 2 * part + cc) * LANE
                kvb_ref[:, kcol:kcol + LANE] = y.astype(BF16)
    gate_ref[...] = jax.nn.sigmoid(chunk(base + 12))


def _fox_proj_kernel(x_ref, g_ref, w_ref, qg_ref, kg_ref, bf_ref, bd_ref, tri_ref,
                     q_ref, kv_ref, kvb_ref, logf_ref, c_ref, carry_ref, *, tiles_per_seq):
    i = pl.program_id(0)
    h = _rmsnorm_rows(x_ref[...], g_ref[...]).astype(BF16)
    bd = bd_ref[...]
    qg, kg = qg_ref[...], kg_ref[...]

    def chunk(c):
        return _dot(h, w_ref[:, c * LANE:(c + 1) * LANE])

    nch = HD // LANE
    for c in range(nch):
        y = _head_norm(chunk(c), qg, bd) * QK_SCALE
        q_ref[:, c * LANE:(c + 1) * LANE] = y.astype(q_ref.dtype)
    for c in range(nch):
        y = _head_norm(chunk(nch + c), kg, bd)
        kv_ref[:, c * LANE:(c + 1) * LANE] = y
        kvb_ref[:, c * LANE:(c + 1) * LANE] = y.astype(BF16)
    for c in range(nch, 2 * nch):
        y = chunk(nch + c)
        kv_ref[:, c * LANE:(c + 1) * LANE] = y
        kvb_ref[:, c * LANE:(c + 1) * LANE] = y.astype(BF16)
    logf = _log_sigmoid(chunk(3 * nch) + bf_ref[...])
    logf_ref[...] = logf

    @pl.when(i % tiles_per_seq == 0)
    def _():
        carry_ref[...] = jnp.zeros_like(carry_ref)

    c_tile = _dot_w3(tri_ref[...], logf) + carry_ref[...]
    c_ref[...] = c_tile
    carry_ref[...] = c_tile[c_tile.shape[0] - 1:, :]


def _sb_proj_kernel(x_ref, g_ref, w_ref, q_ref, kv_ref, kvb_ref):
    h = _rmsnorm_rows(x_ref[...], g_ref[...]).astype(BF16)
    nch = HD // LANE
    for c in range(nch):
        y = _dot(h, w_ref[:, c * LANE:(c + 1) * LANE]) * QK_SCALE
        q_ref[:, c * LANE:(c + 1) * LANE] = y.astype(q_ref.dtype)
    for c in range(nch, 3 * nch):
        y = _dot(h, w_ref[:, c * LANE:(c + 1) * LANE])
        kv_ref[:, (c - nch) * LANE:(c - nch + 1) * LANE] = y
        kvb_ref[:, (c - nch) * LANE:(c - nch + 1) * LANE] = y.astype(BF16)


def _pad_cols(w, n):
    return jnp.pad(w, ((0, 0), (0, n - w.shape[1])))


def _head_avg_matrix():
    r = np.arange(LANE)
    return jnp.asarray(((r[:, None] // HEAD_DIM) == (r[None, :] // HEAD_DIM)).astype(np.float32) / HEAD_DIM, BF16)


def _pair_gain(gain):
    return jnp.tile(gain.reshape(1, HEAD_DIM), (1, LANE // HEAD_DIM)).astype(F32)


def _rope_tables(pos):
    half = ROT_DIM // 2
    inv_freq = ROPE_THETA ** (-jnp.arange(half, dtype=F32) * (2.0 / ROT_DIM))
    ang = pos.astype(F32)[:, None] * inv_freq[None, :]
    cos, sin = jnp.cos(ang), jnp.sin(ang)
    t = pos.shape[0]
    one = jnp.ones((t, HEAD_DIM - ROT_DIM), F32)
    zero_h = jnp.zeros((t, half), F32)
    zero_r = jnp.zeros((t, HEAD_DIM - ROT_DIM), F32)
    cos_h = jnp.concatenate([cos, cos, one], axis=1)
    s_lo_h = jnp.concatenate([-sin, zero_h, zero_r], axis=1)
    s_hi_h = jnp.concatenate([zero_h, sin, zero_r], axis=1)
    rep = LANE // HEAD_DIM
    return jnp.tile(cos_h, (1, rep)), jnp.tile(s_lo_h, (1, rep)), jnp.tile(s_hi_h, (1, rep))


def _proj_tm(m):
    return 256 if m % 256 == 0 else m


def _nsa_project(x, g, w_in, q_gain, k_gain, pos, q_dtype):
    m, d = x.shape
    tm = _proj_tm(m)
    n_in = 21 * LANE
    w = _pad_cols(w_in, n_in).astype(BF16)
    cos, s_lo, s_hi = _rope_tables(pos)
    p = pos.shape[0]
    if p == 1:
        cos, s_lo, s_hi = (jnp.broadcast_to(t, (tm, LANE)) for t in (cos, s_lo, s_hi))
        tab_spec = _const_spec((tm, LANE))
    else:
        per = p // tm
        tab_spec = pl.BlockSpec((tm, LANE), lambda i: (i % per, 0))
    row = lambda n: pl.BlockSpec((tm, n), lambda i: (i, 0))
    outs = pl.pallas_call(
        _nsa_proj_kernel,
        grid=(m // tm,),
        in_specs=[row(d), _const_spec((1, d)), _const_spec((d, n_in)), _const_spec((1, LANE)), _const_spec((1, LANE)),
                  tab_spec, tab_spec, tab_spec, _const_spec((LANE, LANE))],
        out_specs=[row(HD), row(4 * NSA_KVD), row(2 * NSA_KVD), row(6 * NSA_KVD), row(LANE)],
        out_shape=[jax.ShapeDtypeStruct((m, HD), q_dtype),
                   jax.ShapeDtypeStruct((m, 4 * NSA_KVD), F32),
                   jax.ShapeDtypeStruct((m, 2 * NSA_KVD), F32),
                   jax.ShapeDtypeStruct((m, 6 * NSA_KVD), BF16),
                   jax.ShapeDtypeStruct((m, LANE), F32)],
        compiler_params=_cp(("parallel",)),
        name="nsa_proj",
    )(x, g.reshape(1, d), w, _pair_gain(q_gain), _pair_gain(k_gain), cos, s_lo, s_hi, _head_avg_matrix())
    return outs


def _fox_project(x, g, w_in, b_f, q_gain, k_gain, seq_len, q_dtype):
    m, d = x.shape
    tm = _proj_tm(m)
    n_in = 25 * LANE
    w = _pad_cols(w_in, n_in).astype(BF16)
    bf = jnp.pad(b_f.reshape(1, N_HEADS), ((0, 0), (0, LANE - N_HEADS))).astype(F32)
    tri = jnp.asarray(np.tril(np.ones((tm, tm), np.float32)), BF16)
    tiles_per_seq = max(seq_len // tm, 1)
    row = lambda n: pl.BlockSpec((tm, n), lambda i: (i, 0))
    outs = pl.pallas_call(
        functools.partial(_fox_proj_kernel, tiles_per_seq=tiles_per_seq),
        grid=(m // tm,),
        in_specs=[row(d), _const_spec((1, d)), _const_spec((d, n_in)), _const_spec((1, LANE)), _const_spec((1, LANE)),
                  _const_spec((1, LANE)), _const_spec((LANE, LANE)), _const_spec((tm, tm))],
        out_specs=[row(HD), row(2 * HD), row(2 * HD), row(LANE), row(LANE)],
        out_shape=[jax.ShapeDtypeStruct((m, HD), q_dtype),
                   jax.ShapeDtypeStruct((m, 2 * HD), F32),
                   jax.ShapeDtypeStruct((m, 2 * HD), BF16),
                   jax.ShapeDtypeStruct((m, LANE), F32),
                   jax.ShapeDtypeStruct((m, LANE), F32)],
        scratch_shapes=[pltpu.VMEM((1, LANE), F32)],
        compiler_params=_cp(("arbitrary",)),
        name="fox_proj",
    )(x, g.reshape(1, d), w, _pair_gain(q_gain), _pair_gain(k_gain), bf, _head_avg_matrix(), tri)
    return outs


def _sb_project(x, g, w_in, q_dtype):
    m, d = x.shape
    tm = _proj_tm(m)
    w = w_in.astype(BF16)
    row = lambda n: pl.BlockSpec((tm, n), lambda i: (i, 0))
    return pl.pallas_call(
        _sb_proj_kernel,
        grid=(m // tm,),
        in_specs=[row(d), _const_spec((1, d)), _const_spec(w.shape)],
        out_specs=[row(HD), row(2 * HD), row(2 * HD)],
        out_shape=[jax.ShapeDtypeStruct((m, HD), q_dtype),
                   jax.ShapeDtypeStruct((m, 2 * HD), F32),
                   jax.ShapeDtypeStruct((m, 2 * HD), BF16)],
        compiler_params=_cp(("parallel",)),
        name="sb_proj",
    )(x, g.reshape(1, d), w)


def _split_heads(q, lane):
    zero = jnp.zeros_like(q)
    return jnp.concatenate([jnp.where(lane < HEAD_DIM, q, zero), jnp.where(lane >= HEAD_DIM, q, zero)], axis=0)


def _fox_attn_kernel(q_ref, k_ref, v_ref, ct_ref, o_ref, qs_ref, m_ref, l_ref, acc_ref, *, t, rc):
    i = pl.program_id(2)
    lane = lax.broadcasted_iota(jnp.int32, (t, LANE), 1)
    qs_ref[...] = _split_heads(q_ref[0], lane)
    m_ref[...] = jnp.full_like(m_ref, NEG_INF)
    l_ref[...] = jnp.zeros_like(l_ref)
    acc_ref[...] = jnp.zeros_like(acc_ref)

    def tile(off, diagonal):
        k = k_ref[0, pl.ds(off, t), :]
        v = v_ref[0, pl.ds(off, t), :]
        ct = ct_ref[0, 0, :, pl.ds(off, t)]
        for r0 in range(0, 2 * t, rc):
            rows = slice(r0, r0 + rc)
            head = r0 // t
            s = _nt_dot(qs_ref[rows, :], k) - ct[head:head + 1, :]
            if diagonal:
                rr = lax.broadcasted_iota(jnp.int32, (rc, t), 0) + (r0 % t)
                cc = lax.broadcasted_iota(jnp.int32, (rc, t), 1)
                s = jnp.where(cc <= rr, s, NEG_INF)
            m_old = m_ref[rows, :]
            m_new = jnp.maximum(m_old, jnp.max(s, axis=-1, keepdims=True))
            p = jnp.exp(s - m_new)
            alpha = jnp.exp(m_old - m_new)
            l_ref[rows, :] = alpha * l_ref[rows, :] + jnp.sum(p, axis=-1, keepdims=True)
            acc_ref[rows, :] = alpha * acc_ref[rows, :] + _dot(p.astype(BF16), v)
            m_ref[rows, :] = m_new

    def body(j, carry):
        tile(pl.multiple_of(j * t, t), False)
        return carry

    lax.fori_loop(0, i, body, 0)
    tile(pl.multiple_of(i * t, t), True)
    o = acc_ref[...] / l_ref[...]
    o_ref[0] = jnp.where(lane < HEAD_DIM, o[:t], o[t:]).astype(o_ref.dtype)


def _fox_attention(q, kvb, c, n, s):
    t = min(512, s)
    npair = HD // LANE
    ct = c[:, :N_HEADS].reshape(n, s, npair, 2).transpose(0, 2, 3, 1)
    return pl.pallas_call(
        functools.partial(_fox_attn_kernel, t=t, rc=min(256, t)),
        grid=(n, npair, s // t),
        in_specs=[
            pl.BlockSpec((1, t, LANE), lambda b, p, i: (b, i, p)),
            pl.BlockSpec((1, s, LANE), lambda b, p, i: (b, 0, p)),
            pl.BlockSpec((1, s, LANE), lambda b, p, i: (b, 0, npair + p)),
            pl.BlockSpec((1, 1, 2, s), lambda b, p, i: (b, p, 0, 0)),
        ],
        out_specs=pl.BlockSpec((1, t, LANE), lambda b, p, i: (b, i, p)),
        out_shape=jax.ShapeDtypeStruct((n, s, HD), BF16),
        scratch_shapes=[pltpu.VMEM((2 * t, LANE), BF16), pltpu.VMEM((2 * t, 1), F32), pltpu.VMEM((2 * t, 1), F32),
                        pltpu.VMEM((2 * t, LANE), F32)],
        compiler_params=_cp(("parallel", "parallel", "arbitrary")),
        name="fox_attn",
    )(q.reshape(n, s, HD), kvb.reshape(n, s, 2 * HD), kvb.reshape(n, s, 2 * HD), ct).reshape(n * s, HD)


def _sb_attn_kernel(q_ref, k_ref, v_ref, u_ref, o_ref, qs_ref, r_ref, acc_ref, *, t):
    i = pl.program_id(2)
    lane = lax.broadcasted_iota(jnp.int32, (t, LANE), 1)
    qs_ref[...] = _split_heads(q_ref[0], lane)
    r_ref[...] = jnp.zeros_like(r_ref)
    acc_ref[...] = jnp.zeros_like(acc_ref)
    row = lax.broadcasted_iota(jnp.int32, (t, t), 0) + i * t
    col = lax.broadcasted_iota(jnp.int32, (t, t), 1)

    def body(carry):
        jj, _ = carry
        off = pl.multiple_of((i - jj) * t, t)
        k = k_ref[0, pl.ds(off, t), :]
        v = v_ref[0, pl.ds(off, t), :]
        z = _nt_dot(qs_ref[...], k).reshape(2, t, t)
        valid = ((col + off) < row)[None]
        lg = jnp.log1p(jnp.exp(-jnp.abs(z)))
        ls = jnp.where(valid, -(jnp.maximum(z, 0.0) + lg), 0.0).reshape(2 * t, t)
        later = _dot_x2(ls, u_ref[...]) + r_ref[...]
        loga = (jnp.minimum(z, 0.0) - lg).reshape(2 * t, t) + later
        a = jnp.where(jnp.broadcast_to(valid, (2, t, t)).reshape(2 * t, t), jnp.exp(loga), 0.0)
        acc_ref[...] += _dot(a.astype(BF16), v)
        r_new = r_ref[...] + jnp.sum(ls, axis=-1, keepdims=True)
        r_ref[...] = r_new
        return jj + 1, jnp.max(r_new)

    def more(carry):
        jj, r_max = carry
        return (jj <= i) & (r_max > F32_EXP_ZERO)

    lax.while_loop(more, body, (jnp.int32(0), jnp.float32(0.0)))
    o = acc_ref[...]
    o_ref[0] = jnp.where(lane < HEAD_DIM, o[:t], o[t:]).astype(o_ref.dtype)


def _suffix_matrix(t):
    r = np.arange(t)
    return jnp.asarray((r[:, None] > r[None, :]).astype(np.float32), BF16)


def _sb_attention(q, kvb, n, s):
    t = min(256, s)
    npair = HD // LANE
    return pl.pallas_call(
        functools.partial(_sb_attn_kernel, t=t),
        grid=(n, npair, s // t),
        in_specs=[
            pl.BlockSpec((1, t, LANE), lambda b, p, i: (b, i, p)),
            pl.BlockSpec((1, s, LANE), lambda b, p, i: (b, 0, p)),
            pl.BlockSpec((1, s, LANE), lambda b, p, i: (b, 0, npair + p)),
            _const_spec((t, t)),
        ],
        out_specs=pl.BlockSpec((1, t, LANE), lambda b, p, i: (b, i, p)),
        out_shape=jax.ShapeDtypeStruct((n, s, HD), BF16),
        scratch_shapes=[pltpu.VMEM((2 * t, LANE), BF16), pltpu.VMEM((2 * t, 1), F32), pltpu.VMEM((2 * t, LANE), F32)],
        compiler_params=_cp(("parallel", "parallel", "arbitrary")),
        name="sb_attn",
    )(q.reshape(n, s, HD), kvb.reshape(n, s, 2 * HD), kvb.reshape(n, s, 2 * HD), _suffix_matrix(t)).reshape(n * s, HD)


def _cmp_kernel(x_ref, pe_ref, w1_ref, w2_ref, o_ref):
    g, length, width = x_ref.shape
    nb = length // CMP_BLOCK

    def body(p, acc):
        x = x_ref[:, pl.ds(p, nb, stride=CMP_BLOCK), :] + pe_ref[0, pl.ds(p, 1), :]
        return acc + _dot(x.reshape(g * nb, width).astype(BF16), w1_ref[0, p])

    acc = lax.fori_loop(0, CMP_BLOCK, body, jnp.zeros((g * nb, width), F32))
    hid = (acc * jax.nn.sigmoid(acc)).astype(BF16)
    o_ref[:, 0] = _dot(hid, w2_ref[0]).reshape(g, nb, width).astype(o_ref.dtype)


def _cmp_weights(cmp_pos, cmp_w1, cmp_w2):
    rep = LANE // HEAD_DIM
    eye = jnp.eye(rep, dtype=F32)
    w1 = cmp_w1.reshape(2, CMP_BLOCK, HEAD_DIM, HEAD_DIM)
    w1bd = jnp.einsum('hg,cpde->cphdge', eye, w1).reshape(2, CMP_BLOCK, LANE, LANE).astype(BF16)
    w2bd = jnp.einsum('hg,cde->chdge', eye, cmp_w2).reshape(2, LANE, LANE).astype(BF16)
    pe = jnp.tile(cmp_pos, (1, 1, rep)).astype(F32)
    return pe, w1bd, w2bd


def _nsa_compress(rows, n, length, group, cmp_w, out_dtype):
    pe, w1bd, w2bd = cmp_w
    nb = length // CMP_BLOCK
    halves = NSA_KVD // LANE
    return pl.pallas_call(
        _cmp_kernel,
        grid=(n // group, 2, halves),
        in_specs=[
            pl.BlockSpec((group, length, LANE), lambda b, c, f: (b, 0, c * halves + f)),
            pl.BlockSpec((1, CMP_BLOCK, LANE), lambda b, c, f: (c, 0, 0)),
            pl.BlockSpec((1, CMP_BLOCK, LANE, LANE), lambda b, c, f: (c, 0, 0, 0)),
            pl.BlockSpec((1, LANE, LANE), lambda b, c, f: (c, 0, 0)),
        ],
        out_specs=pl.BlockSpec((group, 1, nb, LANE), lambda b, c, f: (b, c, 0, f)),
        out_shape=jax.ShapeDtypeStruct((n, 2, nb, NSA_KVD), out_dtype),
        compiler_params=_cp(("parallel", "arbitrary", "arbitrary")),
        name="nsa_compress",
    )(rows, pe, w1bd, w2bd)


def _masked_softmax_rows(s, mask):
    s = jnp.where(mask, s, NEG_INF)
    m = jnp.max(s, axis=-1, keepdims=True)
    e = jnp.where(mask, jnp.exp(s - m), 0.0)
    return e / jnp.maximum(jnp.sum(e, axis=-1, keepdims=True), 1e-30)


def _both_halves(x, lane, low):
    y = jnp.where((lane < HEAD_DIM) if low else (lane >= HEAD_DIM), x, 0.0)
    return y + pltpu.roll(y, HEAD_DIM, 1)


def _nsa_attn_kernel(q_ref, gate_ref, kc_ref, vc_ref, ks_ref, vs_ref, kw_ref, vw_ref, e_ref, o_ref,
                     q4_ref, sc_ref, m_ref, l_ref, acc_ref, *, tq, tk, wl):
    h = pl.program_id(1)
    i = pl.program_id(2)
    s0 = i * tq
    lane = lax.broadcasted_iota(jnp.int32, (tq, LANE), 1)
    mine = (lane // HEAD_DIM) == (h % 2)

    for c in range(2):
        qc = q_ref[0, :, c * LANE:(c + 1) * LANE].astype(F32)
        q4_ref[(2 * c) * tq:(2 * c + 1) * tq, :] = jnp.where(mine, _both_halves(qc, lane, True), 0.0).astype(BF16)
        q4_ref[(2 * c + 1) * tq:(2 * c + 2) * tq, :] = jnp.where(mine, _both_halves(qc, lane, False), 0.0).astype(BF16)
    q4 = q4_ref[...]

    kc = kc_ref[0, 0]
    nc = kc.shape[0]
    t_c = s0 + lax.broadcasted_iota(jnp.int32, (tq, nc), 0)
    b_c = lax.broadcasted_iota(jnp.int32, (tq, nc), 1)
    cm = ((b_c + 1) * CMP_BLOCK - 1 <= t_c)[None]
    p_c = _masked_softmax_rows(_nt_dot(q4, kc).reshape(4, tq, nc), cm)
    o_c = _dot(p_c.reshape(4 * tq, nc).astype(BF16), vc_ref[0, 0])

    blk = lax.broadcasted_iota(jnp.int32, (nc, tq), 0)
    t_t = s0 + lax.broadcasted_iota(jnp.int32, (nc, tq), 1)
    cm_t = (blk + 1) * CMP_BLOCK - 1 <= t_t
    cm4 = jnp.concatenate([cm_t] * 4, axis=1)
    s_t = jnp.where(cm4, _nt_dot(kc, q4), NEG_INF)
    e_t = jnp.where(cm4, jnp.exp(s_t - jnp.max(s_t, axis=0, keepdims=True)), 0.0)
    p_t = e_t / jnp.maximum(jnp.sum(e_t, axis=0, keepdims=True), 1e-30)
    imp = p_t[:, 0:tq] + p_t[:, tq:2 * tq] + p_t[:, 2 * tq:3 * tq] + p_t[:, 3 * tq:4 * tq]
    cur = t_t // SEL_BLOCK
    valid = blk <= cur
    forced = valid & ((blk == 0) | (blk == cur) | (blk == cur - 1))
    score = jnp.where(forced, FORCE_SCORE, jnp.where(valid, imp, NEG_INF))
    sc_ref[...] = score

    def rank_body(j, rank):
        r = sc_ref[pl.ds(j, 1), :]
        beats = (r > score) | ((r == score) & (j < blk))
        return rank + beats.astype(jnp.int32)

    n_blk = (s0 + tq - 1) // SEL_BLOCK + 1
    rank = lax.fori_loop(0, n_blk, rank_body, jnp.zeros((nc, tq), jnp.int32))
    sel = jnp.where((rank < TOP_N) & valid, 1.0, 0.0).T.astype(BF16)

    m_ref[...] = jnp.full_like(m_ref, NEG_INF)
    l_ref[...] = jnp.zeros_like(l_ref)
    acc_ref[...] = jnp.zeros_like(acc_ref)
    t_s = s0 + lax.broadcasted_iota(jnp.int32, (tq, tk), 0)
    k_s = lax.broadcasted_iota(jnp.int32, (tq, tk), 1)

    def slc_body(kt, carry):
        off = pl.multiple_of(kt * tk, tk)
        k = ks_ref[0, pl.ds(off, tk), :]
        v = vs_ref[0, pl.ds(off, tk), :]
        in_sel = _dot(sel, e_ref[:, pl.ds(off, tk)])
        bias = jnp.where((in_sel > 0.5) & (k_s + off <= t_s), 0.0, NEG_INF)
        for g in range(NSA_GROUP):
            rows = slice(g * tq, (g + 1) * tq)
            s = _nt_dot(q4_ref[rows, :], k) + bias
            m_old = m_ref[rows, :]
            m_new = jnp.maximum(m_old, jnp.max(s, axis=-1, keepdims=True))
            p = jnp.exp(s - m_new)
            alpha = jnp.exp(m_old - m_new)
            l_ref[rows, :] = alpha * l_ref[rows, :] + jnp.sum(p, axis=-1, keepdims=True)
            acc_ref[rows, :] = alpha * acc_ref[rows, :] + _dot(p.astype(BF16), v)
            m_ref[rows, :] = m_new
        return carry

    lax.fori_loop(0, (s0 + tq + tk - 1) // tk, slc_body, 0)
    o_s = acc_ref[...] / l_ref[...]

    st = pl.multiple_of(jnp.maximum(s0 - WINDOW, 0), tq)
    t_w = s0 + lax.broadcasted_iota(jnp.int32, (tq, wl), 0)
    k_w = st + lax.broadcasted_iota(jnp.int32, (tq, wl), 1)
    dist = t_w - k_w
    wm = ((dist >= 0) & (dist <= WINDOW))[None]
    p_w = _masked_softmax_rows(_nt_dot(q4, kw_ref[0, pl.ds(st, wl), :]).reshape(4, tq, wl), wm)
    o_w = _dot(p_w.reshape(4 * tq, wl).astype(BF16), vw_ref[0, pl.ds(st, wl), :])

    n_gate = NSA_GROUP * 3
    r_i = lax.broadcasted_iota(jnp.int32, (LANE, n_gate * LANE), 0)
    l_i = lax.broadcasted_iota(jnp.int32, (LANE, n_gate * LANE), 1)
    onehot = jnp.where(r_i == n_gate * h + l_i // LANE, 1.0, 0.0).astype(BF16)
    gx = _dot_x2(gate_ref[0], onehot)

    outs = []
    for g in range(NSA_GROUP):
        rs = slice(g * tq, (g + 1) * tq)
        gc, gs, gw = (gx[:, (3 * g + b) * LANE:(3 * g + b + 1) * LANE] for b in range(3))
        og = jnp.where(mine, gc * o_c[rs] + gs * o_s[rs] + gw * o_w[rs], 0.0)
        outs.append(og + pltpu.roll(og, HEAD_DIM, 1))
    for c in range(2):
        o_ref[0, :, c * LANE:(c + 1) * LANE] = jnp.where(lane < HEAD_DIM, outs[2 * c], outs[2 * c + 1]).astype(o_ref.dtype)


def _block_expand_matrix(nb, length):
    return jnp.asarray((np.arange(nb)[:, None] == (np.arange(length)[None, :] // SEL_BLOCK)).astype(np.float32), BF16)


def _nsa_attention(q, gates, kvb, cmp_kv, n, s):
    tq = 256
    tk = min(512, s)
    wl = min(WINDOW + tq, s)
    nc = s // CMP_BLOCK
    kv_spec = lambda chunk0: pl.BlockSpec((1, s, LANE), lambda b, h, i: (b, 0, chunk0 + h // 2))
    kvb3 = kvb.reshape(n, s, 6 * NSA_KVD)
    return pl.pallas_call(
        functools.partial(_nsa_attn_kernel, tq=tq, tk=tk, wl=wl),
        grid=(n, NSA_KV_HEADS, s // tq),
        in_specs=[
            pl.BlockSpec((1, tq, NSA_GROUP * HEAD_DIM), lambda b, h, i: (b, i, h)),
            pl.BlockSpec((1, tq, LANE), lambda b, h, i: (b, i, 0)),
            pl.BlockSpec((1, 1, nc, LANE), lambda b, h, i: (b, 0, 0, h // 2)),
            pl.BlockSpec((1, 1, nc, LANE), lambda b, h, i: (b, 1, 0, h // 2)),
            kv_spec(4), kv_spec(6), kv_spec(8), kv_spec(10),
            _const_spec((nc, s)),
        ],
        out_specs=pl.BlockSpec((1, tq, NSA_GROUP * HEAD_DIM), lambda b, h, i: (b, i, h)),
        out_shape=jax.ShapeDtypeStruct((n, s, HD), BF16),
        scratch_shapes=[pltpu.VMEM((4 * tq, LANE), BF16), pltpu.VMEM((nc, tq), F32), pltpu.VMEM((4 * tq, 1), F32),
                        pltpu.VMEM((4 * tq, 1), F32), pltpu.VMEM((4 * tq, LANE), F32)],
        compiler_params=_cp(("parallel", "parallel", "arbitrary")),
        name="nsa_attn",
    )(q.reshape(n, s, HD), gates.reshape(n, s, LANE), cmp_kv, cmp_kv, kvb3, kvb3, kvb3, kvb3,
      _block_expand_matrix(nc, s)).reshape(n * s, HD)


def _gather_kernel(pt_ref, *refs):
    out_ref = refs[-1]
    for k, page_ref in enumerate(refs[:-1]):
        out_ref[0, k] = page_ref[0, 0]


def _gather_pages(cache, page_table, layer):
    nb, n_pages = page_table.shape
    page = cache.shape[2]
    width = int(np.prod(cache.shape[3:]))
    flat = cache.reshape(cache.shape[0], cache.shape[1], page, width)
    per_step = 4 if n_pages % 4 == 0 else 1

    def page_map(b, g, pt, k):
        return (layer, pt[b, g * per_step + k], 0, 0)

    out = pl.pallas_call(
        _gather_kernel,
        grid_spec=pltpu.PrefetchScalarGridSpec(
            num_scalar_prefetch=1,
            grid=(nb, n_pages // per_step),
            in_specs=[pl.BlockSpec((1, 1, page, width), functools.partial(page_map, k=k)) for k in range(per_step)],
            out_specs=pl.BlockSpec((1, per_step, page, width), lambda b, g, pt: (b, g, 0, 0)),
        ),
        out_shape=jax.ShapeDtypeStruct((nb, n_pages, page, width), cache.dtype),
        compiler_params=_cp(("parallel", "arbitrary")),
        name="gather_pages",
    )(page_table, *([flat] * per_step))
    return out.reshape(nb, n_pages * page, width)


def _head_segments(width, n_cols, col_of_head):
    seg = np.zeros((width, n_cols), np.float32)
    for hh in range(width // HEAD_DIM):
        seg[hh * HEAD_DIM:(hh + 1) * HEAD_DIM, col_of_head(hh)] = 1.0
    return seg


def _row8(x):
    return jnp.broadcast_to(x, (8, x.shape[1]))


def _row_expand(x, segt):
    return _dot_x3(_row8(x), segt)[0:1]


def _fox_dec_kernel(q_ref, kvn_ref, lfn_ref, k_ref, v_ref, lf_ref, seg_ref, segt_ref, u_ref, o_ref,
                    m_ref, l_ref, acc_ref, *, r):
    ci = pl.program_id(1)
    n_chunk = pl.num_programs(1)

    @pl.when(ci == 0)
    def _():
        m_ref[...] = jnp.full_like(m_ref, NEG_INF)
        l_ref[...] = jnp.zeros_like(l_ref)
        acc_ref[...] = jnp.zeros_like(acc_ref)

    q = q_ref[0]
    seg, segt = seg_ref[...], segt_ref[...]
    lf_all = lf_ref[0]
    rows = lax.broadcasted_iota(jnp.int32, lf_all.shape, 0)
    lo = pl.multiple_of(ci * r, r)
    tail = jnp.sum(jnp.where(rows >= lo + r, lf_all, 0.0), axis=0, keepdims=True) + lfn_ref[0]
    w = _dot_w3(u_ref[...], lf_ref[0, pl.ds(lo, r), :]) + tail
    s = _dot((k_ref[0] * q).astype(BF16), seg) + w
    m_old = m_ref[...]
    m_new = jnp.maximum(m_old, jnp.max(s, axis=0, keepdims=True))
    p = jnp.exp(s - m_new)
    alpha = jnp.exp(m_old - m_new)
    l_ref[...] = alpha * l_ref[...] + jnp.sum(p, axis=0, keepdims=True)
    acc_ref[...] = _row_expand(alpha, segt) * acc_ref[...] + jnp.sum(_dot(p.astype(BF16), segt) * v_ref[0], axis=0, keepdims=True)
    m_ref[...] = m_new

    @pl.when(ci == n_chunk - 1)
    def _():
        kn = kvn_ref[0, :, 0:HD]
        vn = kvn_ref[0, :, HD:2 * HD]
        s_n = _dot(_row8((kn * q).astype(BF16)), seg)[0:1]
        m_f = jnp.maximum(m_ref[...], s_n)
        a_f = jnp.exp(m_ref[...] - m_f)
        p_n = jnp.exp(s_n - m_f)
        l_f = a_f * l_ref[...] + p_n
        acc = _row_expand(a_f, segt) * acc_ref[...] + _row_expand(p_n, segt) * vn
        o_ref[0] = acc * _row_expand(1.0 / l_f, segt)


def _strict_upper(r):
    i = np.arange(r)
    return jnp.asarray((i[None, :] > i[:, None]).astype(np.float32), BF16)


def _fox_decode(q, kv_new, logf_new, kv_past, lf_past):
    nb, length, _ = kv_past.shape
    r = min(1024, length)
    seg = _head_segments(HD, N_HEADS, lambda hh: hh)
    bspec = lambda shape, im: pl.BlockSpec(shape, im)
    return pl.pallas_call(
        functools.partial(_fox_dec_kernel, r=r),
        grid=(nb, length // r),
        in_specs=[
            bspec((1, 1, HD), lambda b, c: (b, 0, 0)),
            bspec((1, 1, 2 * HD), lambda b, c: (b, 0, 0)),
            bspec((1, 1, N_HEADS), lambda b, c: (b, 0, 0)),
            bspec((1, r, HD), lambda b, c: (b, c, 0)),
            bspec((1, r, HD), lambda b, c: (b, c, 1)),
            bspec((1, length, N_HEADS), lambda b, c: (b, 0, 0)),
            _const_spec((HD, N_HEADS)), _const_spec((N_HEADS, HD)), _const_spec((r, r)),
        ],
        out_specs=bspec((1, 1, HD), lambda b, c: (b, 0, 0)),
        out_shape=jax.ShapeDtypeStruct((nb, 1, HD), F32),
        scratch_shapes=[pltpu.VMEM((1, N_HEADS), F32), pltpu.VMEM((1, N_HEADS), F32), pltpu.VMEM((1, HD), F32)],
        compiler_params=_cp(("parallel", "arbitrary")),
        name="fox_decode",
    )(q.reshape(nb, 1, HD), kv_new.reshape(nb, 1, 2 * HD), logf_new.reshape(nb, 1, N_HEADS), kv_past, kv_past, lf_past,
      jnp.asarray(seg, BF16), jnp.asarray(seg.T, BF16), _strict_upper(r)).reshape(nb, HD)


def _sb_dec_kernel(q_ref, k_ref, v_ref, seg_ref, segt_ref, u_ref, o_ref, r_ref, acc_ref):
    ci = pl.program_id(1)

    @pl.when(ci == 0)
    def _():
        r_ref[...] = jnp.zeros_like(r_ref)
        acc_ref[...] = jnp.zeros_like(acc_ref)

    seg, segt = seg_ref[...], segt_ref[...]
    z = _dot((k_ref[0] * q_ref[0]).astype(BF16), seg)
    lg = jnp.log1p(jnp.exp(-jnp.abs(z)))
    ls = -(jnp.maximum(z, 0.0) + lg)
    later = _dot_w3(u_ref[...], ls) + r_ref[...]
    a = jnp.exp(jnp.minimum(z, 0.0) - lg + later)
    acc_ref[...] += jnp.sum(_dot(a.astype(BF16), segt) * v_ref[0], axis=0, keepdims=True)
    r_ref[...] += jnp.sum(ls, axis=0, keepdims=True)
    o_ref[0] = acc_ref[...]


def _sb_decode(q, kv_past):
    nb, length, _ = kv_past.shape
    r = min(1024, length)
    n_chunk = length // r
    seg = _head_segments(HD, N_HEADS, lambda hh: hh)
    return pl.pallas_call(
        _sb_dec_kernel,
        grid=(nb, n_chunk),
        in_specs=[
            pl.BlockSpec((1, 1, HD), lambda b, c: (b, 0, 0)),
            pl.BlockSpec((1, r, HD), lambda b, c: (b, n_chunk - 1 - c, 0)),
            pl.BlockSpec((1, r, HD), lambda b, c: (b, n_chunk - 1 - c, 1)),
            _const_spec((HD, N_HEADS)), _const_spec((N_HEADS, HD)), _const_spec((r, r)),
        ],
        out_specs=pl.BlockSpec((1, 1, HD), lambda b, c: (b, 0, 0)),
        out_shape=jax.ShapeDtypeStruct((nb, 1, HD), F32),
        scratch_shapes=[pltpu.VMEM((1, N_HEADS), F32), pltpu.VMEM((1, HD), F32)],
        compiler_params=_cp(("parallel", "arbitrary")),
        name="sb_decode",
    )(q.reshape(nb, 1, HD), kv_past, kv_past, jnp.asarray(seg, BF16), jnp.asarray(seg.T, BF16),
      _strict_upper(r)).reshape(nb, HD)


def _dec_branch(k, v, qg, segs, segts, mask, k_new, v_new):
    s = None
    for g in range(NSA_GROUP):
        part = _dot((k * qg[g]).astype(BF16), segs[g])
        s = part if s is None else s + part
    if mask is not None:
        s = jnp.where(mask, s, NEG_INF)
    m = jnp.max(s, axis=0, keepdims=True)
    if k_new is not None:
        s_n = None
        for g in range(NSA_GROUP):
            part = _dot(_row8((k_new * qg[g]).astype(BF16)), segs[g])[0:1]
            s_n = part if s_n is None else s_n + part
        m = jnp.maximum(m, s_n)
    e = jnp.exp(s - m)
    if mask is not None:
        e = jnp.where(mask, e, 0.0)
    l = jnp.sum(e, axis=0, keepdims=True)
    if k_new is not None:
        e_n = jnp.exp(s_n - m)
        l = l + e_n
    inv = 1.0 / l
    eb = e.astype(BF16)
    outs = []
    for g in range(NSA_GROUP):
        o = jnp.sum(_dot(eb, segts[g]) * v, axis=0, keepdims=True)
        if k_new is not None:
            o = o + _row_expand(e_n, segts[g]) * v_new
        outs.append(o * _row_expand(inv, segts[g]))
    return outs, e * inv


def _nsa_dec_kernel(q_ref, gx_ref, new_ref, wnew_ref, cmp_ref, ks_ref, vs_ref, win_ref,
                    seg_ref, segt_ref, rep_ref, erep_ref, o_ref, *, n_blk):
    segs = [seg_ref[g] for g in range(NSA_GROUP)]
    segts = [segt_ref[g] for g in range(NSA_GROUP)]
    qg = [q_ref[0, :, g * NSA_KVD:(g + 1) * NSA_KVD] for g in range(NSA_GROUP)]
    ncol = NSA_GROUP * NSA_KV_HEADS

    o_c, p_c = _dec_branch(cmp_ref[0, 0], cmp_ref[0, 1], qg, segs, segts, None, None, None)

    imp = _dot_x3(p_c, rep_ref[...])
    nc = imp.shape[0]
    pad = erep_ref.shape[1] - nc
    imp = jnp.concatenate([imp, jnp.zeros((pad, ncol), F32)], axis=0)
    blk = lax.broadcasted_iota(jnp.int32, imp.shape, 0)
    cur = n_blk - 1
    valid = blk <= cur
    forced = (blk == 0) | (blk == cur) | (blk == cur - 1)
    score = jnp.where(forced, FORCE_SCORE, jnp.where(valid, imp, NEG_INF))
    rank = jnp.zeros(imp.shape, jnp.int32)
    for j in range(n_blk):
        rj = score[j:j + 1, :]
        rank = rank + ((rj > score) | ((rj == score) & (j < blk))).astype(jnp.int32)
    sel = jnp.where((rank < TOP_N) & valid, 1.0, 0.0).astype(BF16)

    in_sel = _dot(erep_ref[...], sel) > 0.5
    k_new = new_ref[0, :, 2 * NSA_KVD:3 * NSA_KVD]
    v_new = new_ref[0, :, 3 * NSA_KVD:4 * NSA_KVD]
    o_s, _ = _dec_branch(ks_ref[0], vs_ref[0], qg, segs, segts, in_sel, k_new, v_new)

    o_w, _ = _dec_branch(win_ref[0, :, 0:NSA_KVD], win_ref[0, :, NSA_KVD:2 * NSA_KVD], qg, segs, segts, None,
                         wnew_ref[0, :, 0:NSA_KVD], wnew_ref[0, :, NSA_KVD:2 * NSA_KVD])

    for g in range(NSA_GROUP):
        sl = slice(g * NSA_KVD, (g + 1) * NSA_KVD)
        o_ref[0, :, sl] = gx_ref[0, 0:1, sl] * o_c[g] + gx_ref[0, 1:2, sl] * o_s[g] + gx_ref[0, 2:3, sl] * o_w[g]


def _group_major(x):
    lead = x.shape[:-1]
    y = x.reshape(lead + (NSA_KV_HEADS, NSA_GROUP, HEAD_DIM))
    return jnp.swapaxes(y, -3, -2).reshape(lead + (HD,))


def _nsa_decode(q, gates, rows_new, win_new, cmp_kv, rows_past, win_buf):
    nb, length, _ = rows_past.shape
    wlen = win_buf.shape[1]
    nc = length // CMP_BLOCK
    n_blk = -(-(length + 1) // SEL_BLOCK)
    n_blk_pad = -(-n_blk // 8) * 8
    ncol = NSA_GROUP * NSA_KV_HEADS
    segs = np.stack([_head_segments(NSA_KVD, ncol, lambda hh, g=g: NSA_KV_HEADS * g + hh) for g in range(NSA_GROUP)])
    rep = (np.arange(ncol)[:, None] % NSA_KV_HEADS == np.arange(ncol)[None, :] % NSA_KV_HEADS).astype(np.float32)
    erep = (np.arange(length)[:, None] // SEL_BLOCK == np.arange(n_blk_pad)[None, :]).astype(np.float32)
    qg = _group_major(q)
    gx = gates[:, :N_HEADS * 3].reshape(nb, N_HEADS, 3)
    gx = jnp.broadcast_to(jnp.swapaxes(gx, 1, 2)[..., None], (nb, 3, N_HEADS, HEAD_DIM))
    gx = _group_major(gx.reshape(nb, 3, HD))
    out = pl.pallas_call(
        functools.partial(_nsa_dec_kernel, n_blk=n_blk),
        grid=(nb,),
        in_specs=[
            pl.BlockSpec((1, 1, HD), lambda b: (b, 0, 0)),
            pl.BlockSpec((1, 3, HD), lambda b: (b, 0, 0)),
            pl.BlockSpec((1, 1, 4 * NSA_KVD), lambda b: (b, 0, 0)),
            pl.BlockSpec((1, 1, 2 * NSA_KVD), lambda b: (b, 0, 0)),
            pl.BlockSpec((1, 2, nc, NSA_KVD), lambda b: (b, 0, 0, 0)),
            pl.BlockSpec((1, length, NSA_KVD), lambda b: (b, 0, 2)),
            pl.BlockSpec((1, length, NSA_KVD), lambda b: (b, 0, 3)),
            pl.BlockSpec((1, wlen, 2 * NSA_KVD), lambda b: (b, 0, 0)),
            _const_spec((NSA_GROUP, NSA_KVD, ncol)), _const_spec((NSA_GROUP, ncol, NSA_KVD)),
            _const_spec((ncol, ncol)), _const_spec((length, n_blk_pad)),
        ],
        out_specs=pl.BlockSpec((1, 1, HD), lambda b: (b, 0, 0)),
        out_shape=jax.ShapeDtypeStruct((nb, 1, HD), F32),
        compiler_params=_cp(("parallel",)),
        name="nsa_decode",
    )(qg.reshape(nb, 1, HD), gx, rows_new.reshape(nb, 1, 4 * NSA_KVD), win_new.reshape(nb, 1, 2 * NSA_KVD), cmp_kv,
      rows_past, rows_past, win_buf, jnp.asarray(segs, BF16), jnp.asarray(segs.transpose(0, 2, 1), BF16),
      jnp.asarray(rep, BF16), jnp.asarray(erep, BF16))
    o = out.reshape(nb, NSA_GROUP, NSA_KV_HEADS, HEAD_DIM)
    return jnp.swapaxes(o, 1, 2).reshape(nb, HD)


def kernel(x_prompt, x_sample, cache_nsa_kv, state_nsa_win, cache_fox_kv, cache_fox_logf, cache_sb_kv, page_table,
           ffn_norm, ffn_w_gate, ffn_w_up, ffn_w_down, mix_norm, nsa_w_in, nsa_q_gain, nsa_k_gain, nsa_cmp_pos,
           nsa_cmp_w1, nsa_cmp_w2, nsa_w_out, fox_w_in, fox_b_f, fox_q_gain, fox_k_gain, fox_w_out, sb_w_in, sb_w_out):
    n, s, d = x_prompt.shape
    nb, dec_seq, _ = x_sample.shape
    assert dec_seq == 1 and d == D_MODEL
    depth = ffn_norm.shape[0]
    page = cache_nsa_kv.shape[2]
    past = page_table.shape[1] * page
    xp = x_prompt.reshape(n * s, d)
    xs = x_sample.reshape(nb, d)
    wg, wu, wd = ffn_w_gate.astype(BF16), ffn_w_up.astype(BF16), ffn_w_down.astype(BF16)
    pos_p = jnp.arange(s, dtype=jnp.int32)
    pos_s = jnp.full((1,), past, jnp.int32)
    out = {k: [] for k in ("nsa_kv_p", "nsa_win_p", "fox_kv_p", "fox_lf_p", "sb_kv_p",
                           "nsa_kv_s", "nsa_win_s", "fox_kv_s", "fox_lf_s", "sb_kv_s")}
    for i in range(depth):
        j = i // 3
        xp = _ffn_half(xp, ffn_norm[i, 0], wg[i, 0], wu[i, 0], wd[i, 0])
        xs = _ffn_half(xs, ffn_norm[i, 0], wg[i, 0], wu[i, 0], wd[i, 0])
        if i % 3 == 0:
            qg, kg = nsa_q_gain[j], nsa_k_gain[j]
            w_out = nsa_w_out[j].astype(BF16)
            cmp_w = _cmp_weights(nsa_cmp_pos[j], nsa_cmp_w1[j], nsa_cmp_w2[j])
            q, rows, win, kvb, gates = _nsa_project(xp, mix_norm[i], nsa_w_in[j], qg, kg, pos_p, BF16)
            cmp_kv = _nsa_compress(rows.reshape(n, s, 4 * NSA_KVD), n, s, 1, cmp_w, BF16)
            xp = _out_proj(xp, _nsa_attention(q, gates, kvb, cmp_kv, n, s), w_out)
            out["nsa_kv_p"].append(rows.reshape(n, s, 4, NSA_KV_HEADS, HEAD_DIM))
            out["nsa_win_p"].append(win.reshape(n, s, 2, NSA_KV_HEADS, HEAD_DIM)[:, s - min(WINDOW, s):])

            q, rows, win, _, gates = _nsa_project(xs, mix_norm[i], nsa_w_in[j], qg, kg, pos_s, F32)
            rows_past = _gather_pages(cache_nsa_kv, page_table, j)
            cmp_kv = _nsa_compress(rows_past, nb, past, 16 if nb % 16 == 0 else 1, cmp_w, F32)
            wlen = state_nsa_win.shape[2]
            win_buf = state_nsa_win[j].reshape(nb, wlen, 2 * NSA_KVD)
            xs = _out_proj(xs, _nsa_decode(q, gates, rows, win, cmp_kv, rows_past, win_buf), w_out)
            out["nsa_kv_s"].append(rows.reshape(nb, 1, 4, NSA_KV_HEADS, HEAD_DIM))
            win_all = jnp.concatenate([win_buf, win.reshape(nb, 1, 2 * NSA_KVD)], axis=1)[:, 1:]
            out["nsa_win_s"].append(win_all.reshape(nb, wlen, 2, NSA_KV_HEADS, HEAD_DIM))
        elif i % 3 == 1:
            qg, kg = fox_q_gain[j], fox_k_gain[j]
            w_out = fox_w_out[j].astype(BF16)
            q, kv, kvb, logf, c = _fox_project(xp, mix_norm[i], fox_w_in[j], fox_b_f[j], qg, kg, s, BF16)
            xp = _out_proj(xp, _fox_attention(q, kvb, c, n, s), w_out)
            out["fox_kv_p"].append(kv.reshape(n, s, 2, N_HEADS, HEAD_DIM))
            out["fox_lf_p"].append(logf[:, :N_HEADS].reshape(n, s, N_HEADS))

            q, kv, _, logf, _ = _fox_project(xs, mix_norm[i], fox_w_in[j], fox_b_f[j], qg, kg, 1, F32)
            kv_past = _gather_pages(cache_fox_kv, page_table, j)
            lf_past = _gather_pages(cache_fox_logf, page_table, j)
            xs = _out_proj(xs, _fox_decode(q, kv, logf[:, :N_HEADS], kv_past, lf_past), w_out)
            out["fox_kv_s"].append(kv.reshape(nb, 1, 2, N_HEADS, HEAD_DIM))
            out["fox_lf_s"].append(logf[:, :N_HEADS].reshape(nb, 1, N_HEADS))
        else:
            w_out = sb_w_out[j].astype(BF16)
            q, kv, kvb = _sb_project(xp, mix_norm[i], sb_w_in[j], BF16)
            xp = _out_proj(xp, _sb_attention(q, kvb, n, s), w_out)
            out["sb_kv_p"].append(kv.reshape(n, s, 2, N_HEADS, HEAD_DIM))

            q, kv, _ = _sb_project(xs, mix_norm[i], sb_w_in[j], F32)
            kv_past = _gather_pages(cache_sb_kv, page_table, j)
            xs = _out_proj(xs, _sb_decode(q, kv_past), w_out)
            out["sb_kv_s"].append(kv.reshape(nb, 1, 2, N_HEADS, HEAD_DIM))
        xp = _ffn_half(xp, ffn_norm[i, 1], wg[i, 1], wu[i, 1], wd[i, 1])
        xs = _ffn_half(xs, ffn_norm[i, 1], wg[i, 1], wu[i, 1], wd[i, 1])
    st = {k: jnp.stack(v) for k, v in out.items()}
    return (xp.reshape(n, s, d), xs.reshape(nb, 1, d),
            st["nsa_kv_p"], st["nsa_win_p"], st["fox_kv_p"], st["fox_lf_p"], st["sb_kv_p"],
            st["nsa_kv_s"], st["nsa_win_s"], st["fox_kv_s"], st["fox_lf_s"], st["sb_kv_s"])
```

```python
import functools

import numpy as np
import jax
import jax.numpy as jnp
from jax import lax
from jax.experimental import pallas as pl
from jax.experimental.pallas import tpu as pltpu

F32 = jnp.float32
BF16 = jnp.bfloat16

D_MODEL = 1024
HEAD_DIM = 64
N_HEADS = 16
NSA_KV_HEADS = 4
NSA_GROUP = 4
ROT_DIM = 16
ROPE_THETA = 500000.0
CMP_BLOCK = 64
SEL_BLOCK = 64
TOP_N = 16
WINDOW = 512
NORM_EPS = 1e-6
FORCE_SCORE = 1e4
NEG_INF = -1e30
HD = N_HEADS * HEAD_DIM
NSA_KVD = NSA_KV_HEADS * HEAD_DIM
QK_SCALE = HEAD_DIM ** -0.5
DEC_SUFFIX_BLOCK = 256
F32_EXP_ZERO = -104.0

LANE = 128
V7X_VMEM_BYTES = 64 * 1024 * 1024
VMEM_LIMIT = 56 * 1024 * 1024


def _cp(sem, vmem=VMEM_LIMIT):
    return pltpu.CompilerParams(dimension_semantics=sem, vmem_limit_bytes=vmem)


def _dot(a, b):
    return jnp.dot(a, b, preferred_element_type=F32)


def _nt_dot(a, b):
    return lax.dot_general(a, b, (((1,), (1,)), ((), ())), preferred_element_type=F32)


def _dot_x2(x, w):
    hi = x.astype(BF16)
    lo = (x - hi.astype(F32)).astype(BF16)
    return _dot(hi, w) + _dot(lo, w)


def _dot_x3(x, w):
    hi = x.astype(BF16)
    r = x - hi.astype(F32)
    mid = r.astype(BF16)
    lo = (r - mid.astype(F32)).astype(BF16)
    return _dot(hi, w) + _dot(mid, w) + _dot(lo, w)


def _dot_w3(w, x):
    hi = x.astype(BF16)
    r = x - hi.astype(F32)
    mid = r.astype(BF16)
    lo = (r - mid.astype(F32)).astype(BF16)
    return _dot(w, hi) + _dot(w, mid) + _dot(w, lo)


def _rmsnorm_rows(x, g):
    ms = jnp.mean(x * x, axis=-1, keepdims=True)
    return x * lax.rsqrt(ms + NORM_EPS) * g


def _log_sigmoid(x):
    return jnp.minimum(x, 0.0) - jnp.log1p(jnp.exp(-jnp.abs(x)))


def _const_spec(shape):
    nd = len(shape)
    return pl.BlockSpec(shape, lambda *_: (0,) * nd)


def _ffn_kernel(x_ref, g_ref, wg_ref, wu_ref, wd_ref, o_ref, *, ff_chunk):
    x = x_ref[...]
    h = _rmsnorm_rows(x, g_ref[...]).astype(BF16)
    n_ff = wg_ref.shape[1]
    acc = None
    for c in range(n_ff // ff_chunk):
        sl = slice(c * ff_chunk, (c + 1) * ff_chunk)
        a = _dot(h, wg_ref[:, sl])
        b = _dot(h, wu_ref[:, sl])
        u = (a * jax.nn.sigmoid(a) * b).astype(BF16)
        part = _dot(u, wd_ref[sl, :])
        acc = part if acc is None else acc + part
    o_ref[...] = x + 0.5 * acc


def _ffn_half(x, g, wg, wu, wd):
    m, d = x.shape
    ff = wg.shape[1]
    tm = 256 if m % 256 == 0 else m
    ff_chunk = ff // 2 if (ff // 2) % LANE == 0 else ff
    return pl.pallas_call(
        functools.partial(_ffn_kernel, ff_chunk=ff_chunk),
        grid=(m // tm,),
        in_specs=[
            pl.BlockSpec((tm, d), lambda i: (i, 0)),
            _const_spec((1, d)),
            _const_spec((d, ff)),
            _const_spec((d, ff)),
            _const_spec((ff, d)),
        ],
        out_specs=pl.BlockSpec((tm, d), lambda i: (i, 0)),
        out_shape=jax.ShapeDtypeStruct((m, d), F32),
        compiler_params=_cp(("parallel",)),
        name="ffn_half",
    )(x, g.reshape(1, d), wg, wu, wd)


def _outproj_kernel(x_ref, o_ref, w_ref, y_ref):
    y_ref[...] = x_ref[...] + _dot(o_ref[...].astype(BF16), w_ref[...])


def _out_proj(x, o, w):
    m, d = x.shape
    tm = 512 if m % 512 == 0 else m
    return pl.pallas_call(
        _outproj_kernel,
        grid=(m // tm,),
        in_specs=[
            pl.BlockSpec((tm, d), lambda i: (i, 0)),
            pl.BlockSpec((tm, o.shape[1]), lambda i: (i, 0)),
            _const_spec(w.shape),
        ],
        out_specs=pl.BlockSpec((tm, d), lambda i: (i, 0)),
        out_shape=jax.ShapeDtypeStruct((m, d), F32),
        compiler_params=_cp(("parallel",)),
        name="out_proj",
    )(x, o, w)


def _head_norm(x, gain, bd):
    ms = _dot_x2(x * x, bd)
    return x * lax.rsqrt(ms + NORM_EPS) * gain


def _rope(y, cos, s_lo, s_hi):
    return y * cos + pltpu.roll(y, 8, 1) * s_hi + pltpu.roll(y, LANE - 8, 1) * s_lo


def _nsa_proj_kernel(x_ref, g_ref, w_ref, qg_ref, kg_ref, cos_ref, slo_ref, shi_ref, bd_ref,
                     q_ref, rows_ref, win_ref, kvb_ref, gate_ref):
    h = _rmsnorm_rows(x_ref[...], g_ref[...]).astype(BF16)
    cos, s_lo, s_hi = cos_ref[...], slo_ref[...], shi_ref[...]
    bd = bd_ref[...]
    qg, kg = qg_ref[...], kg_ref[...]

    def chunk(c):
        return _dot(h, w_ref[:, c * LANE:(c + 1) * LANE])

    for c in range(HD // LANE):
        y = _rope(_head_norm(chunk(c), qg, bd), cos, s_lo, s_hi) * QK_SCALE
        q_ref[:, c * LANE:(c + 1) * LANE] = y.astype(q_ref.dtype)
    base = HD // LANE
    for b in range(3):
        for part in range(2):
            for cc in range(2):
                c = base + 4 * b + 2 * part + cc
                y = chunk(c)
                if part == 0:
                    y = _rope(_head_norm(y, kg, bd), cos, s_lo, s_hi)
                col = (2 * part + cc) * LANE
                if b < 2:
                    rows_ref[:, 2 * b * NSA_KVD + col - 0:2 * b * NSA_KVD + col + LANE] = y
                else:
                    win_ref[:, col:col + LANE] = y
                kcol = (4 * b + 2 * part + cc) * LANE
                kvb_ref[:, kcol:kcol + LANE] = y.astype(BF16)
    gate_ref[...] = jax.nn.sigmoid(chunk(base + 12))


def _fox_proj_kernel(x_ref, g_ref, w_ref, qg_ref, kg_ref, bf_ref, bd_ref, tri_ref,
                     q_ref, kv_ref, kvb_ref, logf_ref, c_ref, carry_ref, *, tiles_per_seq):
    i = pl.program_id(0)
    h = _rmsnorm_rows(x_ref[...], g_ref[...]).astype(BF16)
    bd = bd_ref[...]
    qg, kg = qg_ref[...], kg_ref[...]

    def chunk(c):
        return _dot(h, w_ref[:, c * LANE:(c + 1) * LANE])

    nch = HD // LANE
    for c in range(nch):
        y = _head_norm(chunk(c), qg, bd) * QK_SCALE
        q_ref[:, c * LANE:(c + 1) * LANE] = y.astype(q_ref.dtype)
    for c in range(nch):
        y = _head_norm(chunk(nch + c), kg, bd)
        kv_ref[:, c * LANE:(c + 1) * LANE] = y
        kvb_ref[:, c * LANE:(c + 1) * LANE] = y.astype(BF16)
    for c in range(nch, 2 * nch):
        y = chunk(nch + c)
        kv_ref[:, c * LANE:(c + 1) * LANE] = y
        kvb_ref[:, c * LANE:(c + 1) * LANE] = y.astype(BF16)
    logf = _log_sigmoid(chunk(3 * nch) + bf_ref[...])
    logf_ref[...] = logf

    @pl.when(i % tiles_per_seq == 0)
    def _():
        carry_ref[...] = jnp.zeros_like(carry_ref)

    c_tile = _dot_w3(tri_ref[...], logf) + carry_ref[...]
    c_ref[...] = c_tile
    carry_ref[...] = c_tile[c_tile.shape[0] - 1:, :]


def _sb_proj_kernel(x_ref, g_ref, w_ref, q_ref, kv_ref, kvb_ref):
    h = _rmsnorm_rows(x_ref[...], g_ref[...]).astype(BF16)
    nch = HD // LANE
    for c in range(nch):
        y = _dot(h, w_ref[:, c * LANE:(c + 1) * LANE]) * QK_SCALE
        q_ref[:, c * LANE:(c + 1) * LANE] = y.astype(q_ref.dtype)
    for c in range(nch, 3 * nch):
        y = _dot(h, w_ref[:, c * LANE:(c + 1) * LANE])
        kv_ref[:, (c - nch) * LANE:(c - nch + 1) * LANE] = y
        kvb_ref[:, (c - nch) * LANE:(c - nch + 1) * LANE] = y.astype(BF16)


def _pad_cols(w, n):
    return jnp.pad(w, ((0, 0), (0, n - w.shape[1])))


def _head_avg_matrix():
    r = np.arange(LANE)
    return jnp.asarray(((r[:, None] // HEAD_DIM) == (r[None, :] // HEAD_DIM)).astype(np.float32) / HEAD_DIM, BF16)


def _pair_gain(gain):
    return jnp.tile(gain.reshape(1, HEAD_DIM), (1, LANE // HEAD_DIM)).astype(F32)


def _rope_tables(pos):
    half = ROT_DIM // 2
    inv_freq = ROPE_THETA ** (-jnp.arange(half, dtype=F32) * (2.0 / ROT_DIM))
    ang = pos.astype(F32)[:, None] * inv_freq[None, :]
    cos, sin = jnp.cos(ang), jnp.sin(ang)
    t = pos.shape[0]
    one = jnp.ones((t, HEAD_DIM - ROT_DIM), F32)
    zero_h = jnp.zeros((t, half), F32)
    zero_r = jnp.zeros((t, HEAD_DIM - ROT_DIM), F32)
    cos_h = jnp.concatenate([cos, cos, one], axis=1)
    s_lo_h = jnp.concatenate([-sin, zero_h, zero_r], axis=1)
    s_hi_h = jnp.concatenate([zero_h, sin, zero_r], axis=1)
    rep = LANE // HEAD_DIM
    return jnp.tile(cos_h, (1, rep)), jnp.tile(s_lo_h, (1, rep)), jnp.tile(s_hi_h, (1, rep))


def _proj_tm(m):
    return 256 if m % 256 == 0 else m


def _nsa_project(x, g, w_in, q_gain, k_gain, pos, q_dtype):
    m, d = x.shape
    tm = _proj_tm(m)
    n_in = 21 * LANE
    w = _pad_cols(w_in, n_in).astype(BF16)
    cos, s_lo, s_hi = _rope_tables(pos)
    p = pos.shape[0]
    if p == 1:
        cos, s_lo, s_hi = (jnp.broadcast_to(t, (tm, LANE)) for t in (cos, s_lo, s_hi))
        tab_spec = _const_spec((tm, LANE))
    else:
        per = p // tm
        tab_spec = pl.BlockSpec((tm, LANE), lambda i: (i % per, 0))
    row = lambda n: pl.BlockSpec((tm, n), lambda i: (i, 0))
    outs = pl.pallas_call(
        _nsa_proj_kernel,
        grid=(m // tm,),
        in_specs=[row(d), _const_spec((1, d)), _const_spec((d, n_in)), _const_spec((1, LANE)), _const_spec((1, LANE)),
                  tab_spec, tab_spec, tab_spec, _const_spec((LANE, LANE))],
        out_specs=[row(HD), row(4 * NSA_KVD), row(2 * NSA_KVD), row(6 * NSA_KVD), row(LANE)],
        out_shape=[jax.ShapeDtypeStruct((m, HD), q_dtype),
                   jax.ShapeDtypeStruct((m, 4 * NSA_KVD), F32),
                   jax.ShapeDtypeStruct((m, 2 * NSA_KVD), F32),
                   jax.ShapeDtypeStruct((m, 6 * NSA_KVD), BF16),
                   jax.ShapeDtypeStruct((m, LANE), F32)],
        compiler_params=_cp(("parallel",)),
        name="nsa_proj",
    )(x, g.reshape(1, d), w, _pair_gain(q_gain), _pair_gain(k_gain), cos, s_lo, s_hi, _head_avg_matrix())
    return outs


def _fox_project(x, g, w_in, b_f, q_gain, k_gain, seq_len, q_dtype):
    m, d = x.shape
    tm = _proj_tm(m)
    n_in = 25 * LANE
    w = _pad_cols(w_in, n_in).astype(BF16)
    bf = jnp.pad(b_f.reshape(1, N_HEADS), ((0, 0), (0, LANE - N_HEADS))).astype(F32)
    tri = jnp.asarray(np.tril(np.ones((tm, tm), np.float32)), BF16)
    tiles_per_seq = max(seq_len // tm, 1)
    row = lambda n: pl.BlockSpec((tm, n), lambda i: (i, 0))
    outs = pl.pallas_call(
        functools.partial(_fox_proj_kernel, tiles_per_seq=tiles_per_seq),
        grid=(m // tm,),
        in_specs=[row(d), _const_spec((1, d)), _const_spec((d, n_in)), _const_spec((1, LANE)), _const_spec((1, LANE)),
                  _const_spec((1, LANE)), _const_spec((LANE, LANE)), _const_spec((tm, tm))],
        out_specs=[row(HD), row(2 * HD), row(2 * HD), row(LANE), row(LANE)],
        out_shape=[jax.ShapeDtypeStruct((m, HD), q_dtype),
                   jax.ShapeDtypeStruct((m, 2 * HD), F32),
                   jax.ShapeDtypeStruct((m, 2 * HD), BF16),
                   jax.ShapeDtypeStruct((m, LANE), F32),
                   jax.ShapeDtypeStruct((m, LANE), F32)],
        scratch_shapes=[pltpu.VMEM((1, LANE), F32)],
        compiler_params=_cp(("arbitrary",)),
        name="fox_proj",
    )(x, g.reshape(1, d), w, _pair_gain(q_gain), _pair_gain(k_gain), bf, _head_avg_matrix(), tri)
    return outs


def _sb_project(x, g, w_in, q_dtype):
    m, d = x.shape
    tm = _proj_tm(m)
    w = w_in.astype(BF16)
    row = lambda n: pl.BlockSpec((tm, n), lambda i: (i, 0))
    return pl.pallas_call(
        _sb_proj_kernel,
        grid=(m // tm,),
        in_specs=[row(d), _const_spec((1, d)), _const_spec(w.shape)],
        out_specs=[row(HD), row(2 * HD), row(2 * HD)],
        out_shape=[jax.ShapeDtypeStruct((m, HD), q_dtype),
                   jax.ShapeDtypeStruct((m, 2 * HD), F32),
                   jax.ShapeDtypeStruct((m, 2 * HD), BF16)],
        compiler_params=_cp(("parallel",)),
        name="sb_proj",
    )(x, g.reshape(1, d), w)


def _split_heads(q, lane):
    zero = jnp.zeros_like(q)
    return jnp.concatenate([jnp.where(lane < HEAD_DIM, q, zero), jnp.where(lane >= HEAD_DIM, q, zero)], axis=0)


def _fox_attn_kernel(q_ref, k_ref, v_ref, ct_ref, o_ref, qs_ref, m_ref, l_ref, acc_ref, *, t, rc):
    i = pl.program_id(2)
    lane = lax.broadcasted_iota(jnp.int32, (t, LANE), 1)
    qs_ref[...] = _split_heads(q_ref[0], lane)
    m_ref[...] = jnp.full_like(m_ref, NEG_INF)
    l_ref[...] = jnp.zeros_like(l_ref)
    acc_ref[...] = jnp.zeros_like(acc_ref)

    def tile(off, diagonal):
        k = k_ref[0, pl.ds(off, t), :]
        v = v_ref[0, pl.ds(off, t), :]
        ct = ct_ref[0, 0, :, pl.ds(off, t)]
        for r0 in range(0, 2 * t, rc):
            rows = slice(r0, r0 + rc)
            head = r0 // t
            s = _nt_dot(qs_ref[rows, :], k) - ct[head:head + 1, :]
            if diagonal:
                rr = lax.broadcasted_iota(jnp.int32, (rc, t), 0) + (r0 % t)
                cc = lax.broadcasted_iota(jnp.int32, (rc, t), 1)
                s = jnp.where(cc <= rr, s, NEG_INF)
            m_old = m_ref[rows, :]
            m_new = jnp.maximum(m_old, jnp.max(s, axis=-1, keepdims=True))
            p = jnp.exp(s - m_new)
            alpha = jnp.exp(m_old - m_new)
            l_ref[rows, :] = alpha * l_ref[rows, :] + jnp.sum(p, axis=-1, keepdims=True)
            acc_ref[rows, :] = alpha * acc_ref[rows, :] + _dot(p.astype(BF16), v)
            m_ref[rows, :] = m_new

    def body(j, carry):
        tile(pl.multiple_of(j * t, t), False)
        return carry

    lax.fori_loop(0, i, body, 0)
    tile(pl.multiple_of(i * t, t), True)
    o = acc_ref[...] / l_ref[...]
    o_ref[0] = jnp.where(lane < HEAD_DIM, o[:t], o[t:]).astype(o_ref.dtype)


def _fox_attention(q, kvb, c, n, s):
    t = min(1024, s)
    npair = HD // LANE
    ct = c[:, :N_HEADS].reshape(n, s, npair, 2).transpose(0, 2, 3, 1)
    return pl.pallas_call(
        functools.partial(_fox_attn_kernel, t=t, rc=min(256, t)),
        grid=(n, npair, s // t),
        in_specs=[
            pl.BlockSpec((1, t, LANE), lambda b, p, i: (b, i, p)),
            pl.BlockSpec((1, s, LANE), lambda b, p, i: (b, 0, p)),
            pl.BlockSpec((1, s, LANE), lambda b, p, i: (b, 0, npair + p)),
            pl.BlockSpec((1, 1, 2, s), lambda b, p, i: (b, p, 0, 0)),
        ],
        out_specs=pl.BlockSpec((1, t, LANE), lambda b, p, i: (b, i, p)),
        out_shape=jax.ShapeDtypeStruct((n, s, HD), BF16),
        scratch_shapes=[pltpu.VMEM((2 * t, LANE), BF16), pltpu.VMEM((2 * t, 1), F32), pltpu.VMEM((2 * t, 1), F32),
                        pltpu.VMEM((2 * t, LANE), F32)],
        compiler_params=_cp(("parallel", "parallel", "arbitrary")),
        name="fox_attn",
    )(q.reshape(n, s, HD), kvb.reshape(n, s, 2 * HD), kvb.reshape(n, s, 2 * HD), ct).reshape(n * s, HD)


def _sb_attn_kernel(q_ref, k_ref, v_ref, u_ref, o_ref, qs_ref, r_ref, acc_ref, *, t):
    i = pl.program_id(2)
    lane = lax.broadcasted_iota(jnp.int32, (t, LANE), 1)
    qs_ref[...] = _split_heads(q_ref[0], lane)
    r_ref[...] = jnp.zeros_like(r_ref)
    acc_ref[...] = jnp.zeros_like(acc_ref)
    row = lax.broadcasted_iota(jnp.int32, (t, t), 0) + i * t
    col = lax.broadcasted_iota(jnp.int32, (t, t), 1)

    def body(carry):
        jj, _ = carry
        off = pl.multiple_of((i - jj) * t, t)
        k = k_ref[0, pl.ds(off, t), :]
        v = v_ref[0, pl.ds(off, t), :]
        z = _nt_dot(qs_ref[...], k).reshape(2, t, t)
        valid = ((col + off) < row)[None]
        lg = jnp.log1p(jnp.exp(-jnp.abs(z)))
        ls = jnp.where(valid, -(jnp.maximum(z, 0.0) + lg), 0.0).reshape(2 * t, t)
        later = _dot_x2(ls, u_ref[...]) + r_ref[...]
        loga = (jnp.minimum(z, 0.0) - lg).reshape(2 * t, t) + later
        a = jnp.where(jnp.broadcast_to(valid, (2, t, t)).reshape(2 * t, t), jnp.exp(loga), 0.0)
        acc_ref[...] += _dot(a.astype(BF16), v)
        r_new = r_ref[...] + jnp.sum(ls, axis=-1, keepdims=True)
        r_ref[...] = r_new
        return jj + 1, jnp.max(r_new)

    def more(carry):
        jj, r_max = carry
        return (jj <= i) & (r_max > F32_EXP_ZERO)

    lax.while_loop(more, body, (jnp.int32(0), jnp.float32(0.0)))
    o = acc_ref[...]
    o_ref[0] = jnp.where(lane < HEAD_DIM, o[:t], o[t:]).astype(o_ref.dtype)


def _suffix_matrix(t):
    r = np.arange(t)
    return jnp.asarray((r[:, None] > r[None, :]).astype(np.float32), BF16)


def _sb_attention(q, kvb, n, s):
    t = min(256, s)
    npair = HD // LANE
    return pl.pallas_call(
        functools.partial(_sb_attn_kernel, t=t),
        grid=(n, npair, s // t),
        in_specs=[
            pl.BlockSpec((1, t, LANE), lambda b, p, i: (b, i, p)),
            pl.BlockSpec((1, s, LANE), lambda b, p, i: (b, 0, p)),
            pl.BlockSpec((1, s, LANE), lambda b, p, i: (b, 0, npair + p)),
            _const_spec((t, t)),
        ],
        out_specs=pl.BlockSpec((1, t, LANE), lambda b, p, i: (b, i, p)),
        out_shape=jax.ShapeDtypeStruct((n, s, HD), BF16),
        scratch_shapes=[pltpu.VMEM((2 * t, LANE), BF16), pltpu.VMEM((2 * t, 1), F32), pltpu.VMEM((2 * t, LANE), F32)],
        compiler_params=_cp(("parallel", "parallel", "arbitrary")),
        name="sb_attn",
    )(q.reshape(n, s, HD), kvb.reshape(n, s, 2 * HD), kvb.reshape(n, s, 2 * HD), _suffix_matrix(t)).reshape(n * s, HD)


def _cmp_kernel(x_ref, pe_ref, w1_ref, w2_ref, o_ref):
    g, length, width = x_ref.shape
    nb = length // CMP_BLOCK

    def body(p, acc):
        x = x_ref[:, pl.ds(p, nb, stride=CMP_BLOCK), :] + pe_ref[0, pl.ds(p, 1), :]
        return acc + _dot(x.reshape(g * nb, width).astype(BF16), w1_ref[0, p])

    acc = lax.fori_loop(0, CMP_BLOCK, body, jnp.zeros((g * nb, width), F32))
    hid = (acc * jax.nn.sigmoid(acc)).astype(BF16)
    o_ref[:, 0] = _dot(hid, w2_ref[0]).reshape(g, nb, width).astype(o_ref.dtype)


def _cmp_weights(cmp_pos, cmp_w1, cmp_w2):
    rep = LANE // HEAD_DIM
    eye = jnp.eye(rep, dtype=F32)
    w1 = cmp_w1.reshape(2, CMP_BLOCK, HEAD_DIM, HEAD_DIM)
    w1bd = jnp.einsum('hg,cpde->cphdge', eye, w1).reshape(2, CMP_BLOCK, LANE, LANE).astype(BF16)
    w2bd = jnp.einsum('hg,cde->chdge', eye, cmp_w2).reshape(2, LANE, LANE).astype(BF16)
    pe = jnp.tile(cmp_pos, (1, 1, rep)).astype(F32)
    return pe, w1bd, w2bd


def _nsa_compress(rows, n, length, group, cmp_w, out_dtype):
    pe, w1bd, w2bd = cmp_w
    nb = length // CMP_BLOCK
    halves = NSA_KVD // LANE
    return pl.pallas_call(
        _cmp_kernel,
        grid=(n // group, 2, halves),
        in_specs=[
            pl.BlockSpec((group, length, LANE), lambda b, c, f: (b, 0, c * halves + f)),
            pl.BlockSpec((1, CMP_BLOCK, LANE), lambda b, c, f: (c, 0, 0)),
            pl.BlockSpec((1, CMP_BLOCK, LANE, LANE), lambda b, c, f: (c, 0, 0, 0)),
            pl.BlockSpec((1, LANE, LANE), lambda b, c, f: (c, 0, 0)),
        ],
        out_specs=pl.BlockSpec((group, 1, nb, LANE), lambda b, c, f: (b, c, 0, f)),
        out_shape=jax.ShapeDtypeStruct((n, 2, nb, NSA_KVD), out_dtype),
        compiler_params=_cp(("parallel", "arbitrary", "arbitrary")),
        name="nsa_compress",
    )(rows, pe, w1bd, w2bd)


def _masked_softmax_rows(s, mask):
    s = jnp.where(mask, s, NEG_INF)
    m = jnp.max(s, axis=-1, keepdims=True)
    e = jnp.where(mask, jnp.exp(s - m), 0.0)
    return e / jnp.maximum(jnp.sum(e, axis=-1, keepdims=True), 1e-30)


def _both_halves(x, lane, low):
    y = jnp.where((lane < HEAD_DIM) if low else (lane >= HEAD_DIM), x, 0.0)
    return y + pltpu.roll(y, HEAD_DIM, 1)


def _nsa_attn_kernel(q_ref, gate_ref, kc_ref, vc_ref, ks_ref, vs_ref, kw_ref, vw_ref, e_ref, o_ref,
                     q4_ref, sc_ref, m_ref, l_ref, acc_ref, *, tq, tk, wl):
    h = pl.program_id(1)
    i = pl.program_id(2)
    s0 = i * tq
    lane = lax.broadcasted_iota(jnp.int32, (tq, LANE), 1)
    mine = (lane // HEAD_DIM) == (h % 2)

    for c in range(2):
        qc = q_ref[0, :, c * LANE:(c + 1) * LANE].astype(F32)
        q4_ref[(2 * c) * tq:(2 * c + 1) * tq, :] = jnp.where(mine, _both_halves(qc, lane, True), 0.0).astype(BF16)
        q4_ref[(2 * c + 1) * tq:(2 * c + 2) * tq, :] = jnp.where(mine, _both_halves(qc, lane, False), 0.0).astype(BF16)
    q4 = q4_ref[...]

    kc = kc_ref[0, 0]
    nc = kc.shape[0]
    t_c = s0 + lax.broadcasted_iota(jnp.int32, (tq, nc), 0)
    b_c = lax.broadcasted_iota(jnp.int32, (tq, nc), 1)
    cm = ((b_c + 1) * CMP_BLOCK - 1 <= t_c)[None]
    p_c = _masked_softmax_rows(_nt_dot(q4, kc).reshape(4, tq, nc), cm)
    o_c = _dot(p_c.reshape(4 * tq, nc).astype(BF16), vc_ref[0, 0])

    blk = lax.broadcasted_iota(jnp.int32, (nc, tq), 0)
    t_t = s0 + lax.broadcasted_iota(jnp.int32, (nc, tq), 1)
    cm_t = (blk + 1) * CMP_BLOCK - 1 <= t_t
    cm4 = jnp.concatenate([cm_t] * 4, axis=1)
    s_t = jnp.where(cm4, _nt_dot(kc, q4), NEG_INF)
    e_t = jnp.where(cm4, jnp.exp(s_t - jnp.max(s_t, axis=0, keepdims=True)), 0.0)
    p_t = e_t / jnp.maximum(jnp.sum(e_t, axis=0, keepdims=True), 1e-30)
    imp = p_t[:, 0:tq] + p_t[:, tq:2 * tq] + p_t[:, 2 * tq:3 * tq] + p_t[:, 3 * tq:4 * tq]
    cur = t_t // SEL_BLOCK
    valid = blk <= cur
    forced = valid & ((blk == 0) | (blk == cur) | (blk == cur - 1))
    score = jnp.where(forced, FORCE_SCORE, jnp.where(valid, imp, NEG_INF))
    sc_ref[...] = score

    def rank_body(j, rank):
        r = sc_ref[pl.ds(j, 1), :]
        beats = (r > score) | ((r == score) & (j < blk))
        return rank + beats.astype(jnp.int32)

    n_blk = (s0 + tq - 1) // SEL_BLOCK + 1
    rank = lax.fori_loop(0, n_blk, rank_body, jnp.zeros((nc, tq), jnp.int32))
    sel = jnp.where((rank < TOP_N) & valid, 1.0, 0.0).T.astype(BF16)

    m_ref[...] = jnp.full_like(m_ref, NEG_INF)
    l_ref[...] = jnp.zeros_like(l_ref)
    acc_ref[...] = jnp.zeros_like(acc_ref)
    t_s = s0 + lax.broadcasted_iota(jnp.int32, (tq, tk), 0)
    k_s = lax.broadcasted_iota(jnp.int32, (tq, tk), 1)

    def slc_body(kt, carry):
        off = pl.multiple_of(kt * tk, tk)
        k = ks_ref[0, pl.ds(off, tk), :]
        v = vs_ref[0, pl.ds(off, tk), :]
        in_sel = _dot(sel, e_ref[:, pl.ds(off, tk)])
        bias = jnp.where((in_sel > 0.5) & (k_s + off <= t_s), 0.0, NEG_INF)
        for g in range(NSA_GROUP):
            rows = slice(g * tq, (g + 1) * tq)
            s = _nt_dot(q4_ref[rows, :], k) + bias
            m_old = m_ref[rows, :]
            m_new = jnp.maximum(m_old, jnp.max(s, axis=-1, keepdims=True))
            p = jnp.exp(s - m_new)
            alpha = jnp.exp(m_old - m_new)
            l_ref[rows, :] = alpha * l_ref[rows, :] + jnp.sum(p, axis=-1, keepdims=True)
            acc_ref[rows, :] = alpha * acc_ref[rows, :] + _dot(p.astype(BF16), v)
            m_ref[rows, :] = m_new
        return carry

    lax.fori_loop(0, (s0 + tq + tk - 1) // tk, slc_body, 0)
    o_s = acc_ref[...] / l_ref[...]

    st = pl.multiple_of(jnp.maximum(s0 - WINDOW, 0), tq)
    t_w = s0 + lax.broadcasted_iota(jnp.int32, (tq, wl), 0)
    k_w = st + lax.broadcasted_iota(jnp.int32, (tq, wl), 1)
    dist = t_w - k_w
    wbias = jnp.where((dist >= 0) & (dist <= WINDOW), 0.0, NEG_INF)
    kw = kw_ref[0, pl.ds(st, wl), :]
    vw = vw_ref[0, pl.ds(st, wl), :]
    o_w = []
    for g in range(NSA_GROUP):
        s_w = _nt_dot(q4_ref[g * tq:(g + 1) * tq, :], kw) + wbias
        e_w = jnp.exp(s_w - jnp.max(s_w, axis=-1, keepdims=True))
        o_w.append(_dot(e_w.astype(BF16), vw) / jnp.sum(e_w, axis=-1, keepdims=True))

    n_gate = NSA_GROUP * 3
    r_i = lax.broadcasted_iota(jnp.int32, (LANE, n_gate * LANE), 0)
    l_i = lax.broadcasted_iota(jnp.int32, (LANE, n_gate * LANE), 1)
    onehot = jnp.where(r_i == n_gate * h + l_i // LANE, 1.0, 0.0).astype(BF16)
    gx = _dot_x2(gate_ref[0], onehot)

    outs = []
    for g in range(NSA_GROUP):
        rs = slice(g * tq, (g + 1) * tq)
        gc, gs, gw = (gx[:, (3 * g + b) * LANE:(3 * g + b + 1) * LANE] for b in range(3))
        og = jnp.where(mine, gc * o_c[rs] + gs * o_s[rs] + gw * o_w[g], 0.0)
        outs.append(og + pltpu.roll(og, HEAD_DIM, 1))
    for c in range(2):
        o_ref[0, :, c * LANE:(c + 1) * LANE] = jnp.where(lane < HEAD_DIM, outs[2 * c], outs[2 * c + 1]).astype(o_ref.dtype)


def _block_expand_matrix(nb, length):
    return jnp.asarray((np.arange(nb)[:, None] == (np.arange(length)[None, :] // SEL_BLOCK)).astype(np.float32), BF16)


def _nsa_attention(q, gates, kvb, cmp_kv, n, s):
    tq = 256
    tk = min(1024, s)
    wl = min(WINDOW + tq, s)
    nc = s // CMP_BLOCK
    kv_spec = lambda chunk0: pl.BlockSpec((1, s, LANE), lambda b, h, i: (b, 0, chunk0 + h // 2))
    kvb3 = kvb.reshape(n, s, 6 * NSA_KVD)
    return pl.pallas_call(
        functools.partial(_nsa_attn_kernel, tq=tq, tk=tk, wl=wl),
        grid=(n, NSA_KV_HEADS, s // tq),
        in_specs=[
            pl.BlockSpec((1, tq, NSA_GROUP * HEAD_DIM), lambda b, h, i: (b, i, h)),
            pl.BlockSpec((1, tq, LANE), lambda b, h, i: (b, i, 0)),
            pl.BlockSpec((1, 1, nc, LANE), lambda b, h, i: (b, 0, 0, h // 2)),
            pl.BlockSpec((1, 1, nc, LANE), lambda b, h, i: (b, 1, 0, h // 2)),
            kv_spec(4), kv_spec(6), kv_spec(8), kv_spec(10),
            _const_spec((nc, s)),
        ],
        out_specs=pl.BlockSpec((1, tq, NSA_GROUP * HEAD_DIM), lambda b, h, i: (b, i, h)),
        out_shape=jax.ShapeDtypeStruct((n, s, HD), BF16),
        scratch_shapes=[pltpu.VMEM((4 * tq, LANE), BF16), pltpu.VMEM((nc, tq), F32), pltpu.VMEM((4 * tq, 1), F32),
                        pltpu.VMEM((4 * tq, 1), F32), pltpu.VMEM((4 * tq, LANE), F32)],
        compiler_params=_cp(("parallel", "parallel", "arbitrary")),
        name="nsa_attn",
    )(q.reshape(n, s, HD), gates.reshape(n, s, LANE), cmp_kv, cmp_kv, kvb3, kvb3, kvb3, kvb3,
      _block_expand_matrix(nc, s)).reshape(n * s, HD)


def _gather_kernel(pt_ref, *refs):
    out_ref = refs[-1]
    for k, page_ref in enumerate(refs[:-1]):
        out_ref[0, k] = page_ref[0, 0]


def _gather_pages(cache, page_table, layer):
    nb, n_pages = page_table.shape
    page = cache.shape[2]
    width = int(np.prod(cache.shape[3:]))
    flat = cache.reshape(cache.shape[0], cache.shape[1], page, width)
    per_step = 4 if n_pages % 4 == 0 else 1

    def page_map(b, g, pt, k):
        return (layer, pt[b, g * per_step + k], 0, 0)

    out = pl.pallas_call(
        _gather_kernel,
        grid_spec=pltpu.PrefetchScalarGridSpec(
            num_scalar_prefetch=1,
            grid=(nb, n_pages // per_step),
            in_specs=[pl.BlockSpec((1, 1, page, width), functools.partial(page_map, k=k)) for k in range(per_step)],
            out_specs=pl.BlockSpec((1, per_step, page, width), lambda b, g, pt: (b, g, 0, 0)),
        ),
        out_shape=jax.ShapeDtypeStruct((nb, n_pages, page, width), cache.dtype),
        compiler_params=_cp(("parallel", "arbitrary")),
        name="gather_pages",
    )(page_table, *([flat] * per_step))
    return out.reshape(nb, n_pages * page, width)


def _head_segments(width, n_cols, col_of_head):
    seg = np.zeros((width, n_cols), np.float32)
    for hh in range(width // HEAD_DIM):
        seg[hh * HEAD_DIM:(hh + 1) * HEAD_DIM, col_of_head(hh)] = 1.0
    return seg


def _row8(x):
    return jnp.broadcast_to(x, (8, x.shape[1]))


def _row_expand(x, segt):
    return _dot_x3(_row8(x), segt)[0:1]


def _suffix_sums(u, x, after):
    cs = u.shape[0]
    parts = []
    for c in reversed(range(x.shape[0] // cs)):
        xc = x[c * cs:(c + 1) * cs, :]
        parts.append(_dot_w3(u, xc) + after)
        after = after + jnp.sum(xc, axis=0, keepdims=True)
    return jnp.concatenate(parts[::-1], axis=0)


def _fox_dec_kernel(q_ref, kvn_ref, lfn_ref, k_ref, v_ref, lf_ref, seg_ref, segt_ref, u_ref, o_ref,
                    m_ref, l_ref, acc_ref, *, r):
    ci = pl.program_id(1)
    n_chunk = pl.num_programs(1)

    @pl.when(ci == 0)
    def _():
        m_ref[...] = jnp.full_like(m_ref, NEG_INF)
        l_ref[...] = jnp.zeros_like(l_ref)
        acc_ref[...] = jnp.zeros_like(acc_ref)

    q = q_ref[0]
    seg, segt = seg_ref[...], segt_ref[...]
    lf_all = lf_ref[0]
    rows = lax.broadcasted_iota(jnp.int32, lf_all.shape, 0)
    lo = pl.multiple_of(ci * r, r)
    tail = jnp.sum(jnp.where(rows >= lo + r, lf_all, 0.0), axis=0, keepdims=True) + lfn_ref[0]
    w = _suffix_sums(u_ref[...], lf_ref[0, pl.ds(lo, r), :], tail)
    s = _dot((k_ref[0] * q).astype(BF16), seg) + w
    m_old = m_ref[...]
    m_new = jnp.maximum(m_old, jnp.max(s, axis=0, keepdims=True))
    p = jnp.exp(s - m_new)
    alpha = jnp.exp(m_old - m_new)
    l_ref[...] = alpha * l_ref[...] + jnp.sum(p, axis=0, keepdims=True)
    acc_ref[...] = _row_expand(alpha, segt) * acc_ref[...] + jnp.sum(_dot(p.astype(BF16), segt) * v_ref[0], axis=0, keepdims=True)
    m_ref[...] = m_new

    @pl.when(ci == n_chunk - 1)
    def _():
        kn = kvn_ref[0, :, 0:HD]
        vn = kvn_ref[0, :, HD:2 * HD]
        s_n = _dot(_row8((kn * q).astype(BF16)), seg)[0:1]
        m_f = jnp.maximum(m_ref[...], s_n)
        a_f = jnp.exp(m_ref[...] - m_f)
        p_n = jnp.exp(s_n - m_f)
        l_f = a_f * l_ref[...] + p_n
        acc = _row_expand(a_f, segt) * acc_ref[...] + _row_expand(p_n, segt) * vn
        o_ref[0] = acc * _row_expand(1.0 / l_f, segt)


def _strict_upper(r):
    i = np.arange(r)
    return jnp.asarray((i[None, :] > i[:, None]).astype(np.float32), BF16)


def _fox_decode(q, kv_new, logf_new, kv_past, lf_past):
    nb, length, _ = kv_past.shape
    r = min(1024, length)
    seg = _head_segments(HD, N_HEADS, lambda hh: hh)
    bspec = lambda shape, im: pl.BlockSpec(shape, im)
    return pl.pallas_call(
        functools.partial(_fox_dec_kernel, r=r),
        grid=(nb, length // r),
        in_specs=[
            bspec((1, 1, HD), lambda b, c: (b, 0, 0)),
            bspec((1, 1, 2 * HD), lambda b, c: (b, 0, 0)),
            bspec((1, 1, N_HEADS), lambda b, c: (b, 0, 0)),
            bspec((1, r, HD), lambda b, c: (b, c, 0)),
            bspec((1, r, HD), lambda b, c: (b, c, 1)),
            bspec((1, length, N_HEADS), lambda b, c: (b, 0, 0)),
            _const_spec((HD, N_HEADS)), _const_spec((N_HEADS, HD)), _const_spec((min(DEC_SUFFIX_BLOCK, r),) * 2),
        ],
        out_specs=bspec((1, 1, HD), lambda b, c: (b, 0, 0)),
        out_shape=jax.ShapeDtypeStruct((nb, 1, HD), F32),
        scratch_shapes=[pltpu.VMEM((1, N_HEADS), F32), pltpu.VMEM((1, N_HEADS), F32), pltpu.VMEM((1, HD), F32)],
        compiler_params=_cp(("parallel", "arbitrary")),
        name="fox_decode",
    )(q.reshape(nb, 1, HD), kv_new.reshape(nb, 1, 2 * HD), logf_new.reshape(nb, 1, N_HEADS), kv_past, kv_past, lf_past,
      jnp.asarray(seg, BF16), jnp.asarray(seg.T, BF16), _strict_upper(min(DEC_SUFFIX_BLOCK, r))).reshape(nb, HD)


def _sb_dec_kernel(q_ref, k_ref, v_ref, seg_ref, segt_ref, u_ref, o_ref, r_ref, acc_ref):
    ci = pl.program_id(1)

    @pl.when(ci == 0)
    def _():
        r_ref[...] = jnp.zeros_like(r_ref)
        acc_ref[...] = jnp.zeros_like(acc_ref)

    seg, segt = seg_ref[...], segt_ref[...]
    z = _dot((k_ref[0] * q_ref[0]).astype(BF16), seg)
    lg = jnp.log1p(jnp.exp(-jnp.abs(z)))
    ls = -(jnp.maximum(z, 0.0) + lg)
    later = _suffix_sums(u_ref[...], ls, r_ref[...])
    a = jnp.exp(jnp.minimum(z, 0.0) - lg + later)
    acc_ref[...] += jnp.sum(_dot(a.astype(BF16), segt) * v_ref[0], axis=0, keepdims=True)
    r_ref[...] += jnp.sum(ls, axis=0, keepdims=True)
    o_ref[0] = acc_ref[...]


def _sb_decode(q, kv_past):
    nb, length, _ = kv_past.shape
    r = min(1024, length)
    n_chunk = length // r
    seg = _head_segments(HD, N_HEADS, lambda hh: hh)
    return pl.pallas_call(
        _sb_dec_kernel,
        grid=(nb, n_chunk),
        in_specs=[
            pl.BlockSpec((1, 1, HD), lambda b, c: (b, 0, 0)),
            pl.BlockSpec((1, r, HD), lambda b, c: (b, n_chunk - 1 - c, 0)),
            pl.BlockSpec((1, r, HD), lambda b, c: (b, n_chunk - 1 - c, 1)),
            _const_spec((HD, N_HEADS)), _const_spec((N_HEADS, HD)), _const_spec((min(DEC_SUFFIX_BLOCK, r),) * 2),
        ],
        out_specs=pl.BlockSpec((1, 1, HD), lambda b, c: (b, 0, 0)),
        out_shape=jax.ShapeDtypeStruct((nb, 1, HD), F32),
        scratch_shapes=[pltpu.VMEM((1, N_HEADS), F32), pltpu.VMEM((1, HD), F32)],
        compiler_params=_cp(("parallel", "arbitrary")),
        name="sb_decode",
    )(q.reshape(nb, 1, HD), kv_past, kv_past, jnp.asarray(seg, BF16), jnp.asarray(seg.T, BF16),
      _strict_upper(min(DEC_SUFFIX_BLOCK, r))).reshape(nb, HD)


def _dec_branch(k, v, qg, segs, segts, mask, k_new, v_new):
    s = None
    for g in range(NSA_GROUP):
        part = _dot((k * qg[g]).astype(BF16), segs[g])
        s = part if s is None else s + part
    if mask is not None:
        s = jnp.where(mask, s, NEG_INF)
    m = jnp.max(s, axis=0, keepdims=True)
    if k_new is not None:
        s_n = None
        for g in range(NSA_GROUP):
            part = _dot(_row8((k_new * qg[g]).astype(BF16)), segs[g])[0:1]
            s_n = part if s_n is None else s_n + part
        m = jnp.maximum(m, s_n)
    e = jnp.exp(s - m)
    if mask is not None:
        e = jnp.where(mask, e, 0.0)
    l = jnp.sum(e, axis=0, keepdims=True)
    if k_new is not None:
        e_n = jnp.exp(s_n - m)
        l = l + e_n
    inv = 1.0 / l
    eb = e.astype(BF16)
    outs = []
    for g in range(NSA_GROUP):
        o = jnp.sum(_dot(eb, segts[g]) * v, axis=0, keepdims=True)
        if k_new is not None:
            o = o + _row_expand(e_n, segts[g]) * v_new
        outs.append(o * _row_expand(inv, segts[g]))
    return outs, e * inv


def _nsa_dec_kernel(q_ref, gx_ref, new_ref, wnew_ref, cmp_ref, ks_ref, vs_ref, win_ref,
                    seg_ref, segt_ref, rep_ref, erep_ref, o_ref, *, n_blk):
    segs = [seg_ref[g] for g in range(NSA_GROUP)]
    segts = [segt_ref[g] for g in range(NSA_GROUP)]
    qg = [q_ref[0, :, g * NSA_KVD:(g + 1) * NSA_KVD] for g in range(NSA_GROUP)]
    ncol = NSA_GROUP * NSA_KV_HEADS

    o_c, p_c = _dec_branch(cmp_ref[0, 0], cmp_ref[0, 1], qg, segs, segts, None, None, None)

    imp = _dot_x3(p_c, rep_ref[...])
    nc = imp.shape[0]
    pad = erep_ref.shape[1] - nc
    imp = jnp.concatenate([imp, jnp.zeros((pad, ncol), F32)], axis=0)
    blk = lax.broadcasted_iota(jnp.int32, imp.shape, 0)
    cur = n_blk - 1
    valid = blk <= cur
    forced = (blk == 0) | (blk == cur) | (blk == cur - 1)
    score = jnp.where(forced, FORCE_SCORE, jnp.where(valid, imp, NEG_INF))
    rank = jnp.zeros(imp.shape, jnp.int32)
    for j in range(n_blk):
        rj = score[j:j + 1, :]
        rank = rank + ((rj > score) | ((rj == score) & (j < blk))).astype(jnp.int32)
    sel = jnp.where((rank < TOP_N) & valid, 1.0, 0.0).astype(BF16)

    in_sel = _dot(erep_ref[...], sel) > 0.5
    k_new = new_ref[0, :, 2 * NSA_KVD:3 * NSA_KVD]
    v_new = new_ref[0, :, 3 * NSA_KVD:4 * NSA_KVD]
    o_s, _ = _dec_branch(ks_ref[0], vs_ref[0], qg, segs, segts, in_sel, k_new, v_new)

    o_w, _ = _dec_branch(win_ref[0, :, 0:NSA_KVD], win_ref[0, :, NSA_KVD:2 * NSA_KVD], qg, segs, segts, None,
                         wnew_ref[0, :, 0:NSA_KVD], wnew_ref[0, :, NSA_KVD:2 * NSA_KVD])

    for g in range(NSA_GROUP):
        sl = slice(g * NSA_KVD, (g + 1) * NSA_KVD)
        o_ref[0, :, sl] = gx_ref[0, 0:1, sl] * o_c[g] + gx_ref[0, 1:2, sl] * o_s[g] + gx_ref[0, 2:3, sl] * o_w[g]


def _group_major(x):
    lead = x.shape[:-1]
    y = x.reshape(lead + (NSA_KV_HEADS, NSA_GROUP, HEAD_DIM))
    return jnp.swapaxes(y, -3, -2).reshape(lead + (HD,))


def _nsa_decode(q, gates, rows_new, win_new, cmp_kv, rows_past, win_buf):
    nb, length, _ = rows_past.shape
    wlen = win_buf.shape[1]
    nc = length // CMP_BLOCK
    n_blk = -(-(length + 1) // SEL_BLOCK)
    n_blk_pad = -(-n_blk // 8) * 8
    ncol = NSA_GROUP * NSA_KV_HEADS
    segs = np.stack([_head_segments(NSA_KVD, ncol, lambda hh, g=g: NSA_KV_HEADS * g + hh) for g in range(NSA_GROUP)])
    rep = (np.arange(ncol)[:, None] % NSA_KV_HEADS == np.arange(ncol)[None, :] % NSA_KV_HEADS).astype(np.float32)
    erep = (np.arange(length)[:, None] // SEL_BLOCK == np.arange(n_blk_pad)[None, :]).astype(np.float32)
    qg = _group_major(q)
    gx = gates[:, :N_HEADS * 3].reshape(nb, N_HEADS, 3)
    gx = jnp.broadcast_to(jnp.swapaxes(gx, 1, 2)[..., None], (nb, 3, N_HEADS, HEAD_DIM))
    gx = _group_major(gx.reshape(nb, 3, HD))
    out = pl.pallas_call(
        functools.partial(_nsa_dec_kernel, n_blk=n_blk),
        grid=(nb,),
        in_specs=[
            pl.BlockSpec((1, 1, HD), lambda b: (b, 0, 0)),
            pl.BlockSpec((1, 3, HD), lambda b: (b, 0, 0)),
            pl.BlockSpec((1, 1, 4 * NSA_KVD), lambda b: (b, 0, 0)),
            pl.BlockSpec((1, 1, 2 * NSA_KVD), lambda b: (b, 0, 0)),
            pl.BlockSpec((1, 2, nc, NSA_KVD), lambda b: (b, 0, 0, 0)),
            pl.BlockSpec((1, length, NSA_KVD), lambda b: (b, 0, 2)),
            pl.BlockSpec((1, length, NSA_KVD), lambda b: (b, 0, 3)),
            pl.BlockSpec((1, wlen, 2 * NSA_KVD), lambda b: (b, 0, 0)),
            _const_spec((NSA_GROUP, NSA_KVD, ncol)), _const_spec((NSA_GROUP, ncol, NSA_KVD)),
            _const_spec((ncol, ncol)), _const_spec((length, n_blk_pad)),
        ],
        out_specs=pl.BlockSpec((1, 1, HD), lambda b: (b, 0, 0)),
        out_shape=jax.ShapeDtypeStruct((nb, 1, HD), F32),
        compiler_params=_cp(("parallel",)),
        name="nsa_decode",
    )(qg.reshape(nb, 1, HD), gx, rows_new.reshape(nb, 1, 4 * NSA_KVD), win_new.reshape(nb, 1, 2 * NSA_KVD), cmp_kv,
      rows_past, rows_past, win_buf, jnp.asarray(segs, BF16), jnp.asarray(segs.transpose(0, 2, 1), BF16),
      jnp.asarray(rep, BF16), jnp.asarray(erep, BF16))
    o = out.reshape(nb, NSA_GROUP, NSA_KV_HEADS, HEAD_DIM)
    return jnp.swapaxes(o, 1, 2).reshape(nb, HD)


def kernel(x_prompt, x_sample, cache_nsa_kv, state_nsa_win, cache_fox_kv, cache_fox_logf, cache_sb_kv, page_table,
           ffn_norm, ffn_w_gate, ffn_w_up, ffn_w_down, mix_norm, nsa_w_in, nsa_q_gain, nsa_k_gain, nsa_cmp_pos,
           nsa_cmp_w1, nsa_cmp_w2, nsa_w_out, fox_w_in, fox_b_f, fox_q_gain, fox_k_gain, fox_w_out, sb_w_in, sb_w_out):
    n, s, d = x_prompt.shape
    nb, dec_seq, _ = x_sample.shape
    assert dec_seq == 1 and d == D_MODEL
    depth = ffn_norm.shape[0]
    page = cache_nsa_kv.shape[2]
    past = page_table.shape[1] * page
    xp = x_prompt.reshape(n * s, d)
    xs = x_sample.reshape(nb, d)
    wg, wu, wd = ffn_w_gate.astype(BF16), ffn_w_up.astype(BF16), ffn_w_down.astype(BF16)
    pos_p = jnp.arange(s, dtype=jnp.int32)
    pos_s = jnp.full((1,), past, jnp.int32)
    out = {k: [] for k in ("nsa_kv_p", "nsa_win_p", "fox_kv_p", "fox_lf_p", "sb_kv_p",
                           "nsa_kv_s", "nsa_win_s", "fox_kv_s", "fox_lf_s", "sb_kv_s")}
    for i in range(depth):
        j = i // 3
        xp = _ffn_half(xp, ffn_norm[i, 0], wg[i, 0], wu[i, 0], wd[i, 0])
        xs = _ffn_half(xs, ffn_norm[i, 0], wg[i, 0], wu[i, 0], wd[i, 0])
        if i % 3 == 0:
            qg, kg = nsa_q_gain[j], nsa_k_gain[j]
            w_out = nsa_w_out[j].astype(BF16)
            cmp_w = _cmp_weights(nsa_cmp_pos[j], nsa_cmp_w1[j], nsa_cmp_w2[j])
            q, rows, win, kvb, gates = _nsa_project(xp, mix_norm[i], nsa_w_in[j], qg, kg, pos_p, BF16)
            cmp_kv = _nsa_compress(rows.reshape(n, s, 4 * NSA_KVD), n, s, 1, cmp_w, BF16)
            xp = _out_proj(xp, _nsa_attention(q, gates, kvb, cmp_kv, n, s), w_out)
            out["nsa_kv_p"].append(rows.reshape(n, s, 4, NSA_KV_HEADS, HEAD_DIM))
            out["nsa_win_p"].append(win.reshape(n, s, 2, NSA_KV_HEADS, HEAD_DIM)[:, s - min(WINDOW, s):])

            q, rows, win, _, gates = _nsa_project(xs, mix_norm[i], nsa_w_in[j], qg, kg, pos_s, F32)
            rows_past = _gather_pages(cache_nsa_kv, page_table, j)
            cmp_kv = _nsa_compress(rows_past, nb, past, 16 if nb % 16 == 0 else 1, cmp_w, F32)
            wlen = state_nsa_win.shape[2]
            win_buf = state_nsa_win[j].reshape(nb, wlen, 2 * NSA_KVD)
            xs = _out_proj(xs, _nsa_decode(q, gates, rows, win, cmp_kv, rows_past, win_buf), w_out)
            out["nsa_kv_s"].append(rows.reshape(nb, 1, 4, NSA_KV_HEADS, HEAD_DIM))
            win_all = jnp.concatenate([win_buf, win.reshape(nb, 1, 2 * NSA_KVD)], axis=1)[:, 1:]
            out["nsa_win_s"].append(win_all.reshape(nb, wlen, 2, NSA_KV_HEADS, HEAD_DIM))
        elif i % 3 == 1:
            qg, kg = fox_q_gain[j], fox_k_gain[j]
            w_out = fox_w_out[j].astype(BF16)
            q, kv, kvb, logf, c = _fox_project(xp, mix_norm[i], fox_w_in[j], fox_b_f[j], qg, kg, s, BF16)
            xp = _out_proj(xp, _fox_attention(q, kvb, c, n, s), w_out)
            out["fox_kv_p"].append(kv.reshape(n, s, 2, N_HEADS, HEAD_DIM))
            out["fox_lf_p"].append(logf[:, :N_HEADS].reshape(n, s, N_HEADS))

            q, kv, _, logf, _ = _fox_project(xs, mix_norm[i], fox_w_in[j], fox_b_f[j], qg, kg, 1, F32)
            kv_past = _gather_pages(cache_fox_kv, page_table, j)
            lf_past = _gather_pages(cache_fox_logf, page_table, j)
            xs = _out_proj(xs, _fox_decode(q, kv, logf[:, :N_HEADS], kv_past, lf_past), w_out)
            out["fox_kv_s"].append(kv.reshape(nb, 1, 2, N_HEADS, HEAD_DIM))
            out["fox_lf_s"].append(logf[:, :N_HEADS].reshape(nb, 1, N_HEADS))
        else:
            w_out = sb_w_out[j].astype(BF16)
            q, kv, kvb = _sb_project(xp, mix_norm[i], sb_w_in[j], BF16)
            xp = _out_proj(xp, _sb_attention(q, kvb, n, s), w_out)
            out["sb_kv_p"].append(kv.reshape(n, s, 2, N_HEADS, HEAD_DIM))

            q, kv, _ = _sb_project(xs, mix_norm[i], sb_w_in[j], F32)
            kv_past = _gather_pages(cache_sb_kv, page_table, j)
            xs = _out_proj(xs, _sb_decode(q, kv_past), w_out)
            out["sb_kv_s"].append(kv.reshape(nb, 1, 2, N_HEADS, HEAD_DIM))
        xp = _ffn_half(xp, ffn_norm[i, 1], wg[i, 1], wu[i, 1], wd[i, 1])
        xs = _ffn_half(xs, ffn_norm[i, 1], wg[i, 1], wu[i, 1], wd[i, 1])
    st = {k: jnp.stack(v) for k, v in out.items()}
    return (xp.reshape(n, s, d), xs.reshape(nb, 1, d),
            st["nsa_kv_p"], st["nsa_win_p"], st["fox_kv_p"], st["fox_lf_p"], st["sb_kv_p"],
            st["nsa_kv_s"], st["nsa_win_s"], st["fox_kv_s"], st["fox_lf_s"], st["sb_kv_s"])
```

```python
import functools

import numpy as np
import jax
import jax.numpy as jnp
from jax import lax
from jax.experimental import pallas as pl
from jax.experimental.pallas import tpu as pltpu

F32 = jnp.float32
BF16 = jnp.bfloat16

D_MODEL = 1024
HEAD_DIM = 64
N_HEADS = 16
NSA_KV_HEADS = 4
NSA_GROUP = 4
ROT_DIM = 16
ROPE_THETA = 500000.0
CMP_BLOCK = 64
SEL_BLOCK = 64
TOP_N = 16
WINDOW = 512
NORM_EPS = 1e-6
FORCE_SCORE = 1e4
NEG_INF = -1e30
HD = N_HEADS * HEAD_DIM
NSA_KVD = NSA_KV_HEADS * HEAD_DIM
QK_SCALE = HEAD_DIM ** -0.5
DEC_SUFFIX_BLOCK = 256
F32_EXP_ZERO = -104.0

LANE = 128
V7X_VMEM_BYTES = 64 * 1024 * 1024
VMEM_LIMIT = 56 * 1024 * 1024


def _cp(sem, vmem=VMEM_LIMIT):
    return pltpu.CompilerParams(dimension_semantics=sem, vmem_limit_bytes=vmem)


def _dot(a, b):
    return jnp.dot(a, b, preferred_element_type=F32)


def _nt_dot(a, b):
    return lax.dot_general(a, b, (((1,), (1,)), ((), ())), preferred_element_type=F32)


def _dot_x2(x, w):
    hi = x.astype(BF16)
    lo = (x - hi.astype(F32)).astype(BF16)
    return _dot(hi, w) + _dot(lo, w)


def _dot_x3(x, w):
    hi = x.astype(BF16)
    r = x - hi.astype(F32)
    mid = r.astype(BF16)
    lo = (r - mid.astype(F32)).astype(BF16)
    return _dot(hi, w) + _dot(mid, w) + _dot(lo, w)


def _dot_w3(w, x):
    hi = x.astype(BF16)
    r = x - hi.astype(F32)
    mid = r.astype(BF16)
    lo = (r - mid.astype(F32)).astype(BF16)
    return _dot(w, hi) + _dot(w, mid) + _dot(w, lo)


def _rmsnorm_rows(x, g):
    ms = jnp.mean(x * x, axis=-1, keepdims=True)
    return x * lax.rsqrt(ms + NORM_EPS) * g


def _log_sigmoid(x):
    return jnp.minimum(x, 0.0) - jnp.log1p(jnp.exp(-jnp.abs(x)))


def _const_spec(shape):
    nd = len(shape)
    return pl.BlockSpec(shape, lambda *_: (0,) * nd)


def _ffn_kernel(x_ref, g_ref, wg_ref, wu_ref, wd_ref, o_ref, *, ff_chunk):
    x = x_ref[...]
    h = _rmsnorm_rows(x, g_ref[...]).astype(BF16)
    n_ff = wg_ref.shape[1]
    acc = None
    for c in range(n_ff // ff_chunk):
        sl = slice(c * ff_chunk, (c + 1) * ff_chunk)
        a = _dot(h, wg_ref[:, sl])
        b = _dot(h, wu_ref[:, sl])
        u = (a * jax.nn.sigmoid(a) * b).astype(BF16)
        part = _dot(u, wd_ref[sl, :])
        acc = part if acc is None else acc + part
    o_ref[...] = x + 0.5 * acc


def _ffn_half(x, g, wg, wu, wd):
    m, d = x.shape
    ff = wg.shape[1]
    tm = 256 if m % 256 == 0 else m
    ff_chunk = ff // 2 if (ff // 2) % LANE == 0 else ff
    return pl.pallas_call(
        functools.partial(_ffn_kernel, ff_chunk=ff_chunk),
        grid=(m // tm,),
        in_specs=[
            pl.BlockSpec((tm, d), lambda i: (i, 0)),
            _const_spec((1, d)),
            _const_spec((d, ff)),
            _const_spec((d, ff)),
            _const_spec((ff, d)),
        ],
        out_specs=pl.BlockSpec((tm, d), lambda i: (i, 0)),
        out_shape=jax.ShapeDtypeStruct((m, d), F32),
        compiler_params=_cp(("parallel",)),
        name="ffn_half",
    )(x, g.reshape(1, d), wg, wu, wd)


def _outproj_kernel(x_ref, o_ref, w_ref, y_ref):
    y_ref[...] = x_ref[...] + _dot(o_ref[...].astype(BF16), w_ref[...])


def _out_proj(x, o, w):
    m, d = x.shape
    tm = 512 if m % 512 == 0 else m
    return pl.pallas_call(
        _outproj_kernel,
        grid=(m // tm,),
        in_specs=[
            pl.BlockSpec((tm, d), lambda i: (i, 0)),
            pl.BlockSpec((tm, o.shape[1]), lambda i: (i, 0)),
            _const_spec(w.shape),
        ],
        out_specs=pl.BlockSpec((tm, d), lambda i: (i, 0)),
        out_shape=jax.ShapeDtypeStruct((m, d), F32),
        compiler_params=_cp(("parallel",)),
        name="out_proj",
    )(x, o, w)


def _head_norm(x, gain, bd):
    ms = _dot_x2(x * x, bd)
    return x * lax.rsqrt(ms + NORM_EPS) * gain


def _rope(y, cos, s_lo, s_hi):
    return y * cos + pltpu.roll(y, 8, 1) * s_hi + pltpu.roll(y, LANE - 8, 1) * s_lo


def _nsa_proj_kernel(x_ref, g_ref, w_ref, qg_ref, kg_ref, cos_ref, slo_ref, shi_ref, bd_ref,
                     q_ref, rows_ref, win_ref, kvb_ref, gate_ref):
    h = _rmsnorm_rows(x_ref[...], g_ref[...]).astype(BF16)
    cos, s_lo, s_hi = cos_ref[...], slo_ref[...], shi_ref[...]
    bd = bd_ref[...]
    qg, kg = qg_ref[...], kg_ref[...]

    def chunk(c):
        return _dot(h, w_ref[:, c * LANE:(c + 1) * LANE])

    for c in range(HD // LANE):
        y = _rope(_head_norm(chunk(c), qg, bd), cos, s_lo, s_hi) * QK_SCALE
        q_ref[:, c * LANE:(c + 1) * LANE] = y.astype(q_ref.dtype)
    base = HD // LANE
    for b in range(3):
        for part in range(2):
            for cc in range(2):
                c = base + 4 * b + 2 * part + cc
                y = chunk(c)
                if part == 0:
                    y = _rope(_head_norm(y, kg, bd), cos, s_lo, s_hi)
                col = (2 * part + cc) * LANE
                if b < 2:
                    rows_ref[:, 2 * b * NSA_KVD + col - 0:2 * b * NSA_KVD + col + LANE] = y
                else:
                    win_ref[:, col:col + LANE] = y
                kcol = (4 * b + 2 * part + cc) * LANE
                kvb_ref[:, kcol:kcol + LANE] = y.astype(BF16)
    gate_ref[...] = jax.nn.sigmoid(chunk(base + 12))


def _fox_proj_kernel(x_ref, g_ref, w_ref, qg_ref, kg_ref, bf_ref, bd_ref, tri_ref,
                     q_ref, kv_ref, kvb_ref, logf_ref, c_ref, carry_ref, *, tiles_per_seq):
    i = pl.program_id(0)
    h = _rmsnorm_rows(x_ref[...], g_ref[...]).astype(BF16)
    bd = bd_ref[...]
    qg, kg = qg_ref[...], kg_ref[...]

    def chunk(c):
        return _dot(h, w_ref[:, c * LANE:(c + 1) * LANE])

    nch = HD // LANE
    for c in range(nch):
        y = _head_norm(chunk(c), qg, bd) * QK_SCALE
        q_ref[:, c * LANE:(c + 1) * LANE] = y.astype(q_ref.dtype)
    for c in range(nch):
        y = _head_norm(chunk(nch + c), kg, bd)
        kv_ref[:, c * LANE:(c + 1) * LANE] = y
        kvb_ref[:, c * LANE:(c + 1) * LANE] = y.astype(BF16)
    for c in range(nch, 2 * nch):
        y = chunk(nch + c)
        kv_ref[:, c * LANE:(c + 1) * LANE] = y
        kvb_ref[:, c * LANE:(c + 1) * LANE] = y.astype(BF16)
    logf = _log_sigmoid(chunk(3 * nch) + bf_ref[...])
    logf_ref[...] = logf

    @pl.when(i % tiles_per_seq == 0)
    def _():
        carry_ref[...] = jnp.zeros_like(carry_ref)

    c_tile = _dot_w3(tri_ref[...], logf) + carry_ref[...]
    c_ref[...] = c_tile
    carry_ref[...] = c_tile[c_tile.shape[0] - 1:, :]


def _sb_proj_kernel(x_ref, g_ref, w_ref, q_ref, kv_ref, kvb_ref):
    h = _rmsnorm_rows(x_ref[...], g_ref[...]).astype(BF16)
    nch = HD // LANE
    for c in range(nch):
        y = _dot(h, w_ref[:, c * LANE:(c + 1) * LANE]) * QK_SCALE
        q_ref[:, c * LANE:(c + 1) * LANE] = y.astype(q_ref.dtype)
    for c in range(nch, 3 * nch):
        y = _dot(h, w_ref[:, c * LANE:(c + 1) * LANE])
        kv_ref[:, (c - nch) * LANE:(c - nch + 1) * LANE] = y
        kvb_ref[:, (c - nch) * LANE:(c - nch + 1) * LANE] = y.astype(BF16)


def _pad_cols(w, n):
    return jnp.pad(w, ((0, 0), (0, n - w.shape[1])))


def _head_avg_matrix():
    r = np.arange(LANE)
    return jnp.asarray(((r[:, None] // HEAD_DIM) == (r[None, :] // HEAD_DIM)).astype(np.float32) / HEAD_DIM, BF16)


def _pair_gain(gain):
    return jnp.tile(gain.reshape(1, HEAD_DIM), (1, LANE // HEAD_DIM)).astype(F32)


def _rope_tables(pos):
    half = ROT_DIM // 2
    inv_freq = ROPE_THETA ** (-jnp.arange(half, dtype=F32) * (2.0 / ROT_DIM))
    ang = pos.astype(F32)[:, None] * inv_freq[None, :]
    cos, sin = jnp.cos(ang), jnp.sin(ang)
    t = pos.shape[0]
    one = jnp.ones((t, HEAD_DIM - ROT_DIM), F32)
    zero_h = jnp.zeros((t, half), F32)
    zero_r = jnp.zeros((t, HEAD_DIM - ROT_DIM), F32)
    cos_h = jnp.concatenate([cos, cos, one], axis=1)
    s_lo_h = jnp.concatenate([-sin, zero_h, zero_r], axis=1)
    s_hi_h = jnp.concatenate([zero_h, sin, zero_r], axis=1)
    rep = LANE // HEAD_DIM
    return jnp.tile(cos_h, (1, rep)), jnp.tile(s_lo_h, (1, rep)), jnp.tile(s_hi_h, (1, rep))


def _proj_tm(m):
    return 256 if m % 256 == 0 else m


def _nsa_project(x, g, w_in, q_gain, k_gain, pos, q_dtype):
    m, d = x.shape
    tm = _proj_tm(m)
    n_in = 21 * LANE
    w = _pad_cols(w_in, n_in).astype(BF16)
    cos, s_lo, s_hi = _rope_tables(pos)
    p = pos.shape[0]
    if p == 1:
        cos, s_lo, s_hi = (jnp.broadcast_to(t, (tm, LANE)) for t in (cos, s_lo, s_hi))
        tab_spec = _const_spec((tm, LANE))
    else:
        per = p // tm
        tab_spec = pl.BlockSpec((tm, LANE), lambda i: (i % per, 0))
    row = lambda n: pl.BlockSpec((tm, n), lambda i: (i, 0))
    outs = pl.pallas_call(
        _nsa_proj_kernel,
        grid=(m // tm,),
        in_specs=[row(d), _const_spec((1, d)), _const_spec((d, n_in)), _const_spec((1, LANE)), _const_spec((1, LANE)),
                  tab_spec, tab_spec, tab_spec, _const_spec((LANE, LANE))],
        out_specs=[row(HD), row(4 * NSA_KVD), row(2 * NSA_KVD), row(6 * NSA_KVD), row(LANE)],
        out_shape=[jax.ShapeDtypeStruct((m, HD), q_dtype),
                   jax.ShapeDtypeStruct((m, 4 * NSA_KVD), F32),
                   jax.ShapeDtypeStruct((m, 2 * NSA_KVD), F32),
                   jax.ShapeDtypeStruct((m, 6 * NSA_KVD), BF16),
                   jax.ShapeDtypeStruct((m, LANE), F32)],
        compiler_params=_cp(("parallel",)),
        name="nsa_proj",
    )(x, g.reshape(1, d), w, _pair_gain(q_gain), _pair_gain(k_gain), cos, s_lo, s_hi, _head_avg_matrix())
    return outs


def _fox_project(x, g, w_in, b_f, q_gain, k_gain, seq_len, q_dtype):
    m, d = x.shape
    tm = _proj_tm(m)
    n_in = 25 * LANE
    w = _pad_cols(w_in, n_in).astype(BF16)
    bf = jnp.pad(b_f.reshape(1, N_HEADS), ((0, 0), (0, LANE - N_HEADS))).astype(F32)
    tri = jnp.asarray(np.tril(np.ones((tm, tm), np.float32)), BF16)
    tiles_per_seq = max(seq_len // tm, 1)
    row = lambda n: pl.BlockSpec((tm, n), lambda i: (i, 0))
    outs = pl.pallas_call(
        functools.partial(_fox_proj_kernel, tiles_per_seq=tiles_per_seq),
        grid=(m // tm,),
        in_specs=[row(d), _const_spec((1, d)), _const_spec((d, n_in)), _const_spec((1, LANE)), _const_spec((1, LANE)),
                  _const_spec((1, LANE)), _const_spec((LANE, LANE)), _const_spec((tm, tm))],
        out_specs=[row(HD), row(2 * HD), row(2 * HD), row(LANE), row(LANE)],
        out_shape=[jax.ShapeDtypeStruct((m, HD), q_dtype),
                   jax.ShapeDtypeStruct((m, 2 * HD), F32),
                   jax.ShapeDtypeStruct((m, 2 * HD), BF16),
                   jax.ShapeDtypeStruct((m, LANE), F32),
                   jax.ShapeDtypeStruct((m, LANE), F32)],
        scratch_shapes=[pltpu.VMEM((1, LANE), F32)],
        compiler_params=_cp(("arbitrary",)),
        name="fox_proj",
    )(x, g.reshape(1, d), w, _pair_gain(q_gain), _pair_gain(k_gain), bf, _head_avg_matrix(), tri)
    return outs


def _sb_project(x, g, w_in, q_dtype):
    m, d = x.shape
    tm = _proj_tm(m)
    w = w_in.astype(BF16)
    row = lambda n: pl.BlockSpec((tm, n), lambda i: (i, 0))
    return pl.pallas_call(
        _sb_proj_kernel,
        grid=(m // tm,),
        in_specs=[row(d), _const_spec((1, d)), _const_spec(w.shape)],
        out_specs=[row(HD), row(2 * HD), row(2 * HD)],
        out_shape=[jax.ShapeDtypeStruct((m, HD), q_dtype),
                   jax.ShapeDtypeStruct((m, 2 * HD), F32),
                   jax.ShapeDtypeStruct((m, 2 * HD), BF16)],
        compiler_params=_cp(("parallel",)),
        name="sb_proj",
    )(x, g.reshape(1, d), w)


def _split_heads(q, lane):
    zero = jnp.zeros_like(q)
    return jnp.concatenate([jnp.where(lane < HEAD_DIM, q, zero), jnp.where(lane >= HEAD_DIM, q, zero)], axis=0)


def _fox_attn_kernel(q_ref, k_ref, v_ref, ct_ref, o_ref, qs_ref, m_ref, l_ref, acc_ref, *, t, rc):
    i = pl.program_id(2)
    lane = lax.broadcasted_iota(jnp.int32, (t, LANE), 1)
    qs_ref[...] = _split_heads(q_ref[0], lane)
    m_ref[...] = jnp.full_like(m_ref, NEG_INF)
    l_ref[...] = jnp.zeros_like(l_ref)
    acc_ref[...] = jnp.zeros_like(acc_ref)

    def tile(off, diagonal):
        k = k_ref[0, pl.ds(off, t), :]
        v = v_ref[0, pl.ds(off, t), :]
        ct = ct_ref[0, 0, :, pl.ds(off, t)]
        for r0 in range(0, 2 * t, rc):
            rows = slice(r0, r0 + rc)
            head = r0 // t
            s = _nt_dot(qs_ref[rows, :], k) - ct[head:head + 1, :]
            if diagonal:
                rr = lax.broadcasted_iota(jnp.int32, (rc, t), 0) + (r0 % t)
                cc = lax.broadcasted_iota(jnp.int32, (rc, t), 1)
                s = jnp.where(cc <= rr, s, NEG_INF)
            m_old = m_ref[rows, :]
            m_new = jnp.maximum(m_old, jnp.max(s, axis=-1, keepdims=True))
            p = jnp.exp(s - m_new)
            alpha = jnp.exp(m_old - m_new)
            l_ref[rows, :] = alpha * l_ref[rows, :] + jnp.sum(p, axis=-1, keepdims=True)
            acc_ref[rows, :] = alpha * acc_ref[rows, :] + _dot(p.astype(BF16), v)
            m_ref[rows, :] = m_new

    def body(j, carry):
        tile(pl.multiple_of(j * t, t), False)
        return carry

    lax.fori_loop(0, i, body, 0)
    tile(pl.multiple_of(i * t, t), True)
    o = acc_ref[...] / l_ref[...]
    o_ref[0] = jnp.where(lane < HEAD_DIM, o[:t], o[t:]).astype(o_ref.dtype)


def _fox_attention(q, kvb, c, n, s):
    t = min(1024, s)
    npair = HD // LANE
    ct = c[:, :N_HEADS].reshape(n, s, npair, 2).transpose(0, 2, 3, 1)
    return pl.pallas_call(
        functools.partial(_fox_attn_kernel, t=t, rc=min(256, t)),
        grid=(n, npair, s // t),
        in_specs=[
            pl.BlockSpec((1, t, LANE), lambda b, p, i: (b, i, p)),
            pl.BlockSpec((1, s, LANE), lambda b, p, i: (b, 0, p)),
            pl.BlockSpec((1, s, LANE), lambda b, p, i: (b, 0, npair + p)),
            pl.BlockSpec((1, 1, 2, s), lambda b, p, i: (b, p, 0, 0)),
        ],
        out_specs=pl.BlockSpec((1, t, LANE), lambda b, p, i: (b, i, p)),
        out_shape=jax.ShapeDtypeStruct((n, s, HD), BF16),
        scratch_shapes=[pltpu.VMEM((2 * t, LANE), BF16), pltpu.VMEM((2 * t, 1), F32), pltpu.VMEM((2 * t, 1), F32),
                        pltpu.VMEM((2 * t, LANE), F32)],
        compiler_params=_cp(("parallel", "parallel", "arbitrary")),
        name="fox_attn",
    )(q.reshape(n, s, HD), kvb.reshape(n, s, 2 * HD), kvb.reshape(n, s, 2 * HD), ct).reshape(n * s, HD)


def _sb_attn_kernel(q_ref, k_ref, v_ref, u_ref, o_ref, qs_ref, r_ref, acc_ref, *, t):
    i = pl.program_id(2)
    lane = lax.broadcasted_iota(jnp.int32, (t, LANE), 1)
    qs_ref[...] = _split_heads(q_ref[0], lane)
    r_ref[...] = jnp.zeros_like(r_ref)
    acc_ref[...] = jnp.zeros_like(acc_ref)
    row = lax.broadcasted_iota(jnp.int32, (t, t), 0) + i * t
    col = lax.broadcasted_iota(jnp.int32, (t, t), 1)

    def body(carry):
        jj, _ = carry
        off = pl.multiple_of((i - jj) * t, t)
        k = k_ref[0, pl.ds(off, t), :]
        v = v_ref[0, pl.ds(off, t), :]
        z = _nt_dot(qs_ref[...], k).reshape(2, t, t)
        valid = ((col + off) < row)[None]
        lg = jnp.log1p(jnp.exp(-jnp.abs(z)))
        ls = jnp.where(valid, -(jnp.maximum(z, 0.0) + lg), 0.0).reshape(2 * t, t)
        later = _dot_x2(ls, u_ref[...]) + r_ref[...]
        loga = (jnp.minimum(z, 0.0) - lg).reshape(2 * t, t) + later
        a = jnp.where(jnp.broadcast_to(valid, (2, t, t)).reshape(2 * t, t), jnp.exp(loga), 0.0)
        acc_ref[...] += _dot(a.astype(BF16), v)
        r_new = r_ref[...] + jnp.sum(ls, axis=-1, keepdims=True)
        r_ref[...] = r_new
        return jj + 1, jnp.max(r_new)

    def more(carry):
        jj, r_max = carry
        return (jj <= i) & (r_max > F32_EXP_ZERO)

    lax.while_loop(more, body, (jnp.int32(0), jnp.float32(0.0)))
    o = acc_ref[...]
    o_ref[0] = jnp.where(lane < HEAD_DIM, o[:t], o[t:]).astype(o_ref.dtype)


def _suffix_matrix(t):
    r = np.arange(t)
    return jnp.asarray((r[:, None] > r[None, :]).astype(np.float32), BF16)


def _sb_attention(q, kvb, n, s):
    t = min(256, s)
    npair = HD // LANE
    return pl.pallas_call(
        functools.partial(_sb_attn_kernel, t=t),
        grid=(n, npair, s // t),
        in_specs=[
            pl.BlockSpec((1, t, LANE), lambda b, p, i: (b, i, p)),
            pl.BlockSpec((1, s, LANE), lambda b, p, i: (b, 0, p)),
            pl.BlockSpec((1, s, LANE), lambda b, p, i: (b, 0, npair + p)),
            _const_spec((t, t)),
        ],
        out_specs=pl.BlockSpec((1, t, LANE), lambda b, p, i: (b, i, p)),
        out_shape=jax.ShapeDtypeStruct((n, s, HD), BF16),
        scratch_shapes=[pltpu.VMEM((2 * t, LANE), BF16), pltpu.VMEM((2 * t, 1), F32), pltpu.VMEM((2 * t, LANE), F32)],
        compiler_params=_cp(("parallel", "parallel", "arbitrary")),
        name="sb_attn",
    )(q.reshape(n, s, HD), kvb.reshape(n, s, 2 * HD), kvb.reshape(n, s, 2 * HD), _suffix_matrix(t)).reshape(n * s, HD)


def _cmp_kernel(x_ref, pe_ref, w1_ref, w2_ref, o_ref):
    g, length, width = x_ref.shape
    nb = length // CMP_BLOCK

    def body(p, acc):
        x = x_ref[:, pl.ds(p, nb, stride=CMP_BLOCK), :] + pe_ref[0, pl.ds(p, 1), :]
        return acc + _dot(x.reshape(g * nb, width).astype(BF16), w1_ref[0, p])

    acc = lax.fori_loop(0, CMP_BLOCK, body, jnp.zeros((g * nb, width), F32))
    hid = (acc * jax.nn.sigmoid(acc)).astype(BF16)
    o_ref[:, 0] = _dot(hid, w2_ref[0]).reshape(g, nb, width).astype(o_ref.dtype)


def _cmp_weights(cmp_pos, cmp_w1, cmp_w2):
    rep = LANE // HEAD_DIM
    eye = jnp.eye(rep, dtype=F32)
    w1 = cmp_w1.reshape(2, CMP_BLOCK, HEAD_DIM, HEAD_DIM)
    w1bd = jnp.einsum('hg,cpde->cphdge', eye, w1).reshape(2, CMP_BLOCK, LANE, LANE).astype(BF16)
    w2bd = jnp.einsum('hg,cde->chdge', eye, cmp_w2).reshape(2, LANE, LANE).astype(BF16)
    pe = jnp.tile(cmp_pos, (1, 1, rep)).astype(F32)
    return pe, w1bd, w2bd


def _nsa_compress(rows, n, length, group, cmp_w, out_dtype):
    pe, w1bd, w2bd = cmp_w
    nb = length // CMP_BLOCK
    halves = NSA_KVD // LANE
    return pl.pallas_call(
        _cmp_kernel,
        grid=(n // group, 2, halves),
        in_specs=[
            pl.BlockSpec((group, length, LANE), lambda b, c, f: (b, 0, c * halves + f)),
            pl.BlockSpec((1, CMP_BLOCK, LANE), lambda b, c, f: (c, 0, 0)),
            pl.BlockSpec((1, CMP_BLOCK, LANE, LANE), lambda b, c, f: (c, 0, 0, 0)),
            pl.BlockSpec((1, LANE, LANE), lambda b, c, f: (c, 0, 0)),
        ],
        out_specs=pl.BlockSpec((group, 1, nb, LANE), lambda b, c, f: (b, c, 0, f)),
        out_shape=jax.ShapeDtypeStruct((n, 2, nb, NSA_KVD), out_dtype),
        compiler_params=_cp(("parallel", "arbitrary", "arbitrary")),
        name="nsa_compress",
    )(rows, pe, w1bd, w2bd)


def _masked_softmax_rows(s, mask):
    s = jnp.where(mask, s, NEG_INF)
    m = jnp.max(s, axis=-1, keepdims=True)
    e = jnp.where(mask, jnp.exp(s - m), 0.0)
    return e / jnp.maximum(jnp.sum(e, axis=-1, keepdims=True), 1e-30)


def _both_halves(x, lane, low):
    y = jnp.where((lane < HEAD_DIM) if low else (lane >= HEAD_DIM), x, 0.0)
    return y + pltpu.roll(y, HEAD_DIM, 1)


def _nsa_attn_kernel(q_ref, gate_ref, kc_ref, vc_ref, ks_ref, vs_ref, kw_ref, vw_ref, e_ref, o_ref,
                     q4_ref, sc_ref, m_ref, l_ref, acc_ref, *, tq, tk, wl):
    h = pl.program_id(1)
    i = pl.program_id(2)
    s0 = i * tq
    lane = lax.broadcasted_iota(jnp.int32, (tq, LANE), 1)
    mine = (lane // HEAD_DIM) == (h % 2)

    for c in range(2):
        qc = q_ref[0, :, c * LANE:(c + 1) * LANE].astype(F32)
        q4_ref[(2 * c) * tq:(2 * c + 1) * tq, :] = jnp.where(mine, _both_halves(qc, lane, True), 0.0).astype(BF16)
        q4_ref[(2 * c + 1) * tq:(2 * c + 2) * tq, :] = jnp.where(mine, _both_halves(qc, lane, False), 0.0).astype(BF16)
    q4 = q4_ref[...]

    kc = kc_ref[0, 0]
    nc = kc.shape[0]
    t_c = s0 + lax.broadcasted_iota(jnp.int32, (tq, nc), 0)
    b_c = lax.broadcasted_iota(jnp.int32, (tq, nc), 1)
    cm = ((b_c + 1) * CMP_BLOCK - 1 <= t_c)[None]
    p_c = _masked_softmax_rows(_nt_dot(q4, kc).reshape(4, tq, nc), cm)
    o_c = _dot(p_c.reshape(4 * tq, nc).astype(BF16), vc_ref[0, 0])

    blk = lax.broadcasted_iota(jnp.int32, (nc, tq), 0)
    t_t = s0 + lax.broadcasted_iota(jnp.int32, (nc, tq), 1)
    cm_t = (blk + 1) * CMP_BLOCK - 1 <= t_t
    cm4 = jnp.concatenate([cm_t] * 4, axis=1)
    s_t = jnp.where(cm4, _nt_dot(kc, q4), NEG_INF)
    e_t = jnp.where(cm4, jnp.exp(s_t - jnp.max(s_t, axis=0, keepdims=True)), 0.0)
    p_t = e_t / jnp.maximum(jnp.sum(e_t, axis=0, keepdims=True), 1e-30)
    imp = p_t[:, 0:tq] + p_t[:, tq:2 * tq] + p_t[:, 2 * tq:3 * tq] + p_t[:, 3 * tq:4 * tq]
    cur = t_t // SEL_BLOCK
    valid = blk <= cur
    forced = valid & ((blk == 0) | (blk == cur) | (blk == cur - 1))
    score = jnp.where(forced, FORCE_SCORE, jnp.where(valid, imp, NEG_INF))
    sc_ref[...] = score

    def rank_body(j, rank):
        r = sc_ref[pl.ds(j, 1), :]
        beats = (r > score) | ((r == score) & (j < blk))
        return rank + beats.astype(jnp.int32)

    n_blk = (s0 + tq - 1) // SEL_BLOCK + 1
    rank = lax.fori_loop(0, n_blk, rank_body, jnp.zeros((nc, tq), jnp.int32))
    sel = jnp.where((rank < TOP_N) & valid, 1.0, 0.0).T.astype(BF16)

    m_ref[...] = jnp.full_like(m_ref, NEG_INF)
    l_ref[...] = jnp.zeros_like(l_ref)
    acc_ref[...] = jnp.zeros_like(acc_ref)
    t_s = s0 + lax.broadcasted_iota(jnp.int32, (tq, tk), 0)
    k_s = lax.broadcasted_iota(jnp.int32, (tq, tk), 1)

    def slc_body(kt, carry):
        off = pl.multiple_of(kt * tk, tk)
        k = ks_ref[0, pl.ds(off, tk), :]
        v = vs_ref[0, pl.ds(off, tk), :]
        in_sel = _dot(sel, e_ref[:, pl.ds(off, tk)])
        bias = jnp.where((in_sel > 0.5) & (k_s + off <= t_s), 0.0, NEG_INF)
        for g in range(NSA_GROUP):
            rows = slice(g * tq, (g + 1) * tq)
            s = _nt_dot(q4_ref[rows, :], k) + bias
            m_old = m_ref[rows, :]
            m_new = jnp.maximum(m_old, jnp.max(s, axis=-1, keepdims=True))
            p = jnp.exp(s - m_new)
            alpha = jnp.exp(m_old - m_new)
            l_ref[rows, :] = alpha * l_ref[rows, :] + jnp.sum(p, axis=-1, keepdims=True)
            acc_ref[rows, :] = alpha * acc_ref[rows, :] + _dot(p.astype(BF16), v)
            m_ref[rows, :] = m_new
        return carry

    lax.fori_loop(0, (s0 + tq + tk - 1) // tk, slc_body, 0)
    o_s = acc_ref[...] / l_ref[...]

    st = pl.multiple_of(jnp.maximum(s0 - WINDOW, 0), tq)
    t_w = s0 + lax.broadcasted_iota(jnp.int32, (tq, wl), 0)
    k_w = st + lax.broadcasted_iota(jnp.int32, (tq, wl), 1)
    dist = t_w - k_w
    wbias = jnp.where((dist >= 0) & (dist <= WINDOW), 0.0, NEG_INF)
    kw = kw_ref[0, pl.ds(st, wl), :]
    vw = vw_ref[0, pl.ds(st, wl), :]
    o_w = []
    for g in range(NSA_GROUP):
        s_w = _nt_dot(q4_ref[g * tq:(g + 1) * tq, :], kw) + wbias
        e_w = jnp.exp(s_w - jnp.max(s_w, axis=-1, keepdims=True))
        o_w.append(_dot(e_w.astype(BF16), vw) / jnp.sum(e_w, axis=-1, keepdims=True))

    n_gate = NSA_GROUP * 3
    r_i = lax.broadcasted_iota(jnp.int32, (LANE, n_gate * LANE), 0)
    l_i = lax.broadcasted_iota(jnp.int32, (LANE, n_gate * LANE), 1)
    onehot = jnp.where(r_i == n_gate * h + l_i // LANE, 1.0, 0.0).astype(BF16)
    gx = _dot_x2(gate_ref[0], onehot)

    outs = []
    for g in range(NSA_GROUP):
        rs = slice(g * tq, (g + 1) * tq)
        gc, gs, gw = (gx[:, (3 * g + b) * LANE:(3 * g + b + 1) * LANE] for b in range(3))
        og = jnp.where(mine, gc * o_c[rs] + gs * o_s[rs] + gw * o_w[g], 0.0)
        outs.append(og + pltpu.roll(og, HEAD_DIM, 1))
    for c in range(2):
        o_ref[0, :, c * LANE:(c + 1) * LANE] = jnp.where(lane < HEAD_DIM, outs[2 * c], outs[2 * c + 1]).astype(o_ref.dtype)


def _block_expand_matrix(nb, length):
    return jnp.asarray((np.arange(nb)[:, None] == (np.arange(length)[None, :] // SEL_BLOCK)).astype(np.float32), BF16)


def _nsa_attention(q, gates, kvb, cmp_kv, n, s):
    tq = 256
    tk = min(1024, s)
    wl = min(WINDOW + tq, s)
    nc = s // CMP_BLOCK
    kv_spec = lambda chunk0: pl.BlockSpec((1, s, LANE), lambda b, h, i: (b, 0, chunk0 + h // 2))
    kvb3 = kvb.reshape(n, s, 6 * NSA_KVD)
    return pl.pallas_call(
        functools.partial(_nsa_attn_kernel, tq=tq, tk=tk, wl=wl),
        grid=(n, NSA_KV_HEADS, s // tq),
        in_specs=[
            pl.BlockSpec((1, tq, NSA_GROUP * HEAD_DIM), lambda b, h, i: (b, i, h)),
            pl.BlockSpec((1, tq, LANE), lambda b, h, i: (b, i, 0)),
            pl.BlockSpec((1, 1, nc, LANE), lambda b, h, i: (b, 0, 0, h // 2)),
            pl.BlockSpec((1, 1, nc, LANE), lambda b, h, i: (b, 1, 0, h // 2)),
            kv_spec(4), kv_spec(6), kv_spec(8), kv_spec(10),
            _const_spec((nc, s)),
        ],
        out_specs=pl.BlockSpec((1, tq, NSA_GROUP * HEAD_DIM), lambda b, h, i: (b, i, h)),
        out_shape=jax.ShapeDtypeStruct((n, s, HD), BF16),
        scratch_shapes=[pltpu.VMEM((4 * tq, LANE), BF16), pltpu.VMEM((nc, tq), F32), pltpu.VMEM((4 * tq, 1), F32),
                        pltpu.VMEM((4 * tq, 1), F32), pltpu.VMEM((4 * tq, LANE), F32)],
        compiler_params=_cp(("parallel", "parallel", "arbitrary")),
        name="nsa_attn",
    )(q.reshape(n, s, HD), gates.reshape(n, s, LANE), cmp_kv, cmp_kv, kvb3, kvb3, kvb3, kvb3,
      _block_expand_matrix(nc, s)).reshape(n * s, HD)


def _gather_kernel(pt_ref, *refs):
    out_ref = refs[-1]
    for k, page_ref in enumerate(refs[:-1]):
        out_ref[0, k] = page_ref[0, 0]


def _gather_pages(cache, page_table, layer):
    nb, n_pages = page_table.shape
    page = cache.shape[2]
    width = int(np.prod(cache.shape[3:]))
    flat = cache.reshape(cache.shape[0], cache.shape[1], page, width)
    per_step = 4 if n_pages % 4 == 0 else 1

    def page_map(b, g, pt, k):
        return (layer, pt[b, g * per_step + k], 0, 0)

    out = pl.pallas_call(
        _gather_kernel,
        grid_spec=pltpu.PrefetchScalarGridSpec(
            num_scalar_prefetch=1,
            grid=(nb, n_pages // per_step),
            in_specs=[pl.BlockSpec((1, 1, page, width), functools.partial(page_map, k=k)) for k in range(per_step)],
            out_specs=pl.BlockSpec((1, per_step, page, width), lambda b, g, pt: (b, g, 0, 0)),
        ),
        out_shape=jax.ShapeDtypeStruct((nb, n_pages, page, width), cache.dtype),
        compiler_params=_cp(("parallel", "arbitrary")),
        name="gather_pages",
    )(page_table, *([flat] * per_step))
    return out.reshape(nb, n_pages * page, width)


def _head_segments(width, n_cols, col_of_head):
    seg = np.zeros((width, n_cols), np.float32)
    for hh in range(width // HEAD_DIM):
        seg[hh * HEAD_DIM:(hh + 1) * HEAD_DIM, col_of_head(hh)] = 1.0
    return seg


def _row8(x):
    return jnp.broadcast_to(x, (8, x.shape[1]))


def _row_expand(x, segt):
    return _dot_x3(_row8(x), segt)[0:1]


def _suffix_sums(u, x, after):
    cs = u.shape[0]
    parts = []
    for c in reversed(range(x.shape[0] // cs)):
        xc = x[c * cs:(c + 1) * cs, :]
        parts.append(_dot_w3(u, xc) + after)
        after = after + jnp.sum(xc, axis=0, keepdims=True)
    return jnp.concatenate(parts[::-1], axis=0)


def _fox_dec_kernel(q_ref, kvn_ref, lfn_ref, k_ref, v_ref, lf_ref, seg_ref, segt_ref, u_ref, o_ref,
                    m_ref, l_ref, acc_ref, *, r):
    ci = pl.program_id(1)
    n_chunk = pl.num_programs(1)

    @pl.when(ci == 0)
    def _():
        m_ref[...] = jnp.full_like(m_ref, NEG_INF)
        l_ref[...] = jnp.zeros_like(l_ref)
        acc_ref[...] = jnp.zeros_like(acc_ref)

    q = q_ref[0]
    seg, segt = seg_ref[...], segt_ref[...]
    lf_all = lf_ref[0]
    rows = lax.broadcasted_iota(jnp.int32, lf_all.shape, 0)
    lo = pl.multiple_of(ci * r, r)
    tail = jnp.sum(jnp.where(rows >= lo + r, lf_all, 0.0), axis=0, keepdims=True) + lfn_ref[0]
    w = _suffix_sums(u_ref[...], lf_ref[0, pl.ds(lo, r), :], tail)
    s = _dot((k_ref[0] * q).astype(BF16), seg) + w
    m_old = m_ref[...]
    m_new = jnp.maximum(m_old, jnp.max(s, axis=0, keepdims=True))
    p = jnp.exp(s - m_new)
    alpha = jnp.exp(m_old - m_new)
    l_ref[...] = alpha * l_ref[...] + jnp.sum(p, axis=0, keepdims=True)
    acc_ref[...] = _row_expand(alpha, segt) * acc_ref[...] + jnp.sum(_dot(p.astype(BF16), segt) * v_ref[0], axis=0, keepdims=True)
    m_ref[...] = m_new

    @pl.when(ci == n_chunk - 1)
    def _():
        kn = kvn_ref[0, :, 0:HD]
        vn = kvn_ref[0, :, HD:2 * HD]
        s_n = _dot(_row8((kn * q).astype(BF16)), seg)[0:1]
        m_f = jnp.maximum(m_ref[...], s_n)
        a_f = jnp.exp(m_ref[...] - m_f)
        p_n = jnp.exp(s_n - m_f)
        l_f = a_f * l_ref[...] + p_n
        acc = _row_expand(a_f, segt) * acc_ref[...] + _row_expand(p_n, segt) * vn
        o_ref[0] = acc * _row_expand(1.0 / l_f, segt)


def _strict_upper(r):
    i = np.arange(r)
    return jnp.asarray((i[None, :] > i[:, None]).astype(np.float32), BF16)


def _fox_decode(q, kv_new, logf_new, kv_past, lf_past):
    nb, length, _ = kv_past.shape
    r = min(1024, length)
    seg = _head_segments(HD, N_HEADS, lambda hh: hh)
    bspec = lambda shape, im: pl.BlockSpec(shape, im)
    return pl.pallas_call(
        functools.partial(_fox_dec_kernel, r=r),
        grid=(nb, length // r),
        in_specs=[
            bspec((1, 1, HD), lambda b, c: (b, 0, 0)),
            bspec((1, 1, 2 * HD), lambda b, c: (b, 0, 0)),
            bspec((1, 1, N_HEADS), lambda b, c: (b, 0, 0)),
            bspec((1, r, HD), lambda b, c: (b, c, 0)),
            bspec((1, r, HD), lambda b, c: (b, c, 1)),
            bspec((1, length, N_HEADS), lambda b, c: (b, 0, 0)),
            _const_spec((HD, N_HEADS)), _const_spec((N_HEADS, HD)), _const_spec((min(DEC_SUFFIX_BLOCK, r),) * 2),
        ],
        out_specs=bspec((1, 1, HD), lambda b, c: (b, 0, 0)),
        out_shape=jax.ShapeDtypeStruct((nb, 1, HD), F32),
        scratch_shapes=[pltpu.VMEM((1, N_HEADS), F32), pltpu.VMEM((1, N_HEADS), F32), pltpu.VMEM((1, HD), F32)],
        compiler_params=_cp(("parallel", "arbitrary")),
        name="fox_decode",
    )(q.reshape(nb, 1, HD), kv_new.reshape(nb, 1, 2 * HD), logf_new.reshape(nb, 1, N_HEADS), kv_past, kv_past, lf_past,
      jnp.asarray(seg, BF16), jnp.asarray(seg.T, BF16), _strict_upper(min(DEC_SUFFIX_BLOCK, r))).reshape(nb, HD)


def _sb_dec_kernel(q_ref, k_ref, v_ref, seg_ref, segt_ref, u_ref, o_ref, r_ref, acc_ref):
    ci = pl.program_id(1)

    @pl.when(ci == 0)
    def _():
        r_ref[...] = jnp.zeros_like(r_ref)
        acc_ref[...] = jnp.zeros_like(acc_ref)

    seg, segt = seg_ref[...], segt_ref[...]
    z = _dot((k_ref[0] * q_ref[0]).astype(BF16), seg)
    lg = jnp.log1p(jnp.exp(-jnp.abs(z)))
    ls = -(jnp.maximum(z, 0.0) + lg)
    later = _suffix_sums(u_ref[...], ls, r_ref[...])
    a = jnp.exp(jnp.minimum(z, 0.0) - lg + later)
    acc_ref[...] += jnp.sum(_dot(a.astype(BF16), segt) * v_ref[0], axis=0, keepdims=True)
    r_ref[...] += jnp.sum(ls, axis=0, keepdims=True)
    o_ref[0] = acc_ref[...]


def _sb_decode(q, kv_past):
    nb, length, _ = kv_past.shape
    r = min(1024, length)
    n_chunk = length // r
    seg = _head_segments(HD, N_HEADS, lambda hh: hh)
    return pl.pallas_call(
        _sb_dec_kernel,
        grid=(nb, n_chunk),
        in_specs=[
            pl.BlockSpec((1, 1, HD), lambda b, c: (b, 0, 0)),
            pl.BlockSpec((1, r, HD), lambda b, c: (b, n_chunk - 1 - c, 0)),
            pl.BlockSpec((1, r, HD), lambda b, c: (b, n_chunk - 1 - c, 1)),
            _const_spec((HD, N_HEADS)), _const_spec((N_HEADS, HD)), _const_spec((min(DEC_SUFFIX_BLOCK, r),) * 2),
        ],
        out_specs=pl.BlockSpec((1, 1, HD), lambda b, c: (b, 0, 0)),
        out_shape=jax.ShapeDtypeStruct((nb, 1, HD), F32),
        scratch_shapes=[pltpu.VMEM((1, N_HEADS), F32), pltpu.VMEM((1, HD), F32)],
        compiler_params=_cp(("parallel", "arbitrary")),
        name="sb_decode",
    )(q.reshape(nb, 1, HD), kv_past, kv_past, jnp.asarray(seg, BF16), jnp.asarray(seg.T, BF16),
      _strict_upper(min(DEC_SUFFIX_BLOCK, r))).reshape(nb, HD)


DEC_PAGES_PER_STEP = 4


def _paged_scores(page_refs, qb_ref, s_ref):
    for pg, ref in enumerate(page_refs):
        p = ref.shape[-1]
        for h in range(N_HEADS):
            s_ref[h:h + 1, pg * p:(pg + 1) * p] = jnp.sum(ref[0, 0, 0, h] * qb_ref[0, h], axis=0, keepdims=True)


def _paged_values(page_refs, w_ref, acc_ref, alpha):
    for h in range(N_HEADS):
        a = acc_ref[h]
        if alpha is not None:
            a = a * alpha[h:h + 1, :]
        for pg, ref in enumerate(page_refs):
            p = ref.shape[-1]
            a = a + w_ref[h:h + 1, pg * p:(pg + 1) * p] * ref[0, 0, 1, h]
        acc_ref[h] = a


def _lane_total_row(x, ones8):
    hi = x.astype(BF16)
    r = x - hi.astype(F32)
    mid = r.astype(BF16)
    lo = (r - mid.astype(F32)).astype(BF16)
    return (_nt_dot(ones8, hi) + _nt_dot(ones8, mid) + _nt_dot(ones8, lo))[0:1]


def _fox_dect_kernel(pt_ref, qb_ref, q_ref, kvn_ref, vn_ref, lfn_ref, seg_ref, ones16_ref, ones8_ref, ut_ref, *rest):
    n_pg = DEC_PAGES_PER_STEP
    kv_refs, lf_refs = rest[:n_pg], rest[n_pg:2 * n_pg]
    o_ref, s_ref, p_ref, acc_ref, m_ref, l_ref, carry_ref = rest[2 * n_pg:]
    g = pl.program_id(1)

    @pl.when(g == 0)
    def _():
        m_ref[...] = jnp.full_like(m_ref, NEG_INF)
        l_ref[...] = jnp.zeros_like(l_ref)
        acc_ref[...] = jnp.zeros_like(acc_ref)
        carry_ref[...] = lfn_ref[0]

    _paged_scores(kv_refs, qb_ref, s_ref)
    lf = jnp.concatenate([r[0, 0] for r in lf_refs], axis=1)
    w = _dot_x3(lf, ut_ref[...]) + jnp.tile(carry_ref[...], (1, n_pg))
    logits = s_ref[...] + w
    m_old = m_ref[...]
    m_new = jnp.maximum(m_old, jnp.max(logits, axis=-1, keepdims=True))
    p = jnp.exp(logits - jnp.tile(m_new, (1, n_pg)))
    alpha = jnp.exp(m_old - m_new)
    l_ref[...] = alpha * l_ref[...] + jnp.sum(p, axis=-1, keepdims=True)
    p_ref[...] = p
    _paged_values(kv_refs, p_ref, acc_ref, alpha)
    m_ref[...] = m_new
    carry_ref[...] = carry_ref[...] + jnp.sum(lf, axis=-1, keepdims=True)

    @pl.when(g == pl.num_programs(1) - 1)
    def _():
        s_row = _dot(_row8((kvn_ref[0, :, 0:HD] * q_ref[0]).astype(BF16)), seg_ref[...])[0:1]
        eye = lax.broadcasted_iota(jnp.int32, (N_HEADS, N_HEADS), 0) == lax.broadcasted_iota(jnp.int32, (N_HEADS, N_HEADS), 1)
        s_n = _dot_x3(jnp.where(eye, jnp.broadcast_to(s_row, (N_HEADS, N_HEADS)), 0.0), ones16_ref[...])
        m_f = jnp.maximum(m_ref[...], s_n)
        a_f = jnp.exp(m_ref[...] - m_f)
        p_n = jnp.exp(s_n - m_f)
        inv = 1.0 / (a_f * l_ref[...] + p_n)
        for h in range(N_HEADS):
            o_h = _lane_total_row(acc_ref[h], ones8_ref[...])
            hs = slice(h, h + 1)
            o_ref[0, hs, :] = (a_f[hs, 0:HEAD_DIM] * o_h + p_n[hs, 0:HEAD_DIM] * vn_ref[0, hs, :]) * inv[hs, 0:HEAD_DIM]


def _sb_dect_kernel(pt_ref, qb_ref, ones8_ref, ut_ref, *rest):
    n_pg = DEC_PAGES_PER_STEP
    kv_refs = rest[:n_pg]
    o_ref, s_ref, p_ref, acc_ref, carry_ref = rest[n_pg:]
    g = pl.program_id(1)

    @pl.when(g == 0)
    def _():
        acc_ref[...] = jnp.zeros_like(acc_ref)
        carry_ref[...] = jnp.zeros_like(carry_ref)

    _paged_scores(kv_refs, qb_ref, s_ref)
    z = s_ref[...]
    lg = jnp.log1p(jnp.exp(-jnp.abs(z)))
    ls = -(jnp.maximum(z, 0.0) + lg)
    later = _dot_x3(ls, ut_ref[...]) + jnp.tile(carry_ref[...], (1, n_pg))
    p_ref[...] = jnp.exp(jnp.minimum(z, 0.0) - lg + later)
    _paged_values(kv_refs, p_ref, acc_ref, None)
    carry_ref[...] = carry_ref[...] + jnp.sum(ls, axis=-1, keepdims=True)

    @pl.when(g == pl.num_programs(1) - 1)
    def _():
        for h in range(N_HEADS):
            o_ref[0, h:h + 1, :] = _lane_total_row(acc_ref[h], ones8_ref[...])


def _paged_decode(kernel_fn, cache_kv, page_table, layer, q, lead_inputs, lead_specs, cache_lf=None, scratch=()):
    nb, n_pages = page_table.shape
    n_pg = DEC_PAGES_PER_STEP
    assert n_pages % n_pg == 0
    page = cache_kv.shape[2]
    kvt = jnp.transpose(cache_kv, (0, 1, 3, 4, 5, 2))
    qb = jnp.broadcast_to(q.reshape(nb, N_HEADS, HEAD_DIM, 1), (nb, N_HEADS, HEAD_DIM, page))

    def page_map(b, g, pt, k, nd):
        return (layer, pt[b, n_pages - (g + 1) * n_pg + k]) + (0,) * nd

    kv_specs = [pl.BlockSpec((1, 1, 2, N_HEADS, HEAD_DIM, page), functools.partial(page_map, k=k, nd=4)) for k in range(n_pg)]
    inputs = [qb] + list(lead_inputs) + [kvt] * n_pg
    specs = [pl.BlockSpec((1, N_HEADS, HEAD_DIM, page), lambda b, g, pt: (b, 0, 0, 0))] + list(lead_specs) + kv_specs
    if cache_lf is not None:
        lft = jnp.transpose(cache_lf, (0, 1, 3, 2))
        inputs += [lft] * n_pg
        specs += [pl.BlockSpec((1, 1, N_HEADS, page), functools.partial(page_map, k=k, nd=2)) for k in range(n_pg)]
    span = n_pg * page
    out = pl.pallas_call(
        kernel_fn,
        grid_spec=pltpu.PrefetchScalarGridSpec(
            num_scalar_prefetch=1,
            grid=(nb, n_pages // n_pg),
            in_specs=specs,
            out_specs=pl.BlockSpec((1, N_HEADS, HEAD_DIM), lambda b, g, pt: (b, 0, 0)),
            scratch_shapes=[pltpu.VMEM((N_HEADS, span), F32), pltpu.VMEM((N_HEADS, span), F32),
                            pltpu.VMEM((N_HEADS, HEAD_DIM, page), F32)] + list(scratch),
        ),
        out_shape=jax.ShapeDtypeStruct((nb, N_HEADS, HEAD_DIM), F32),
        compiler_params=_cp(("parallel", "arbitrary")),
        name=kernel_fn.__name__.strip("_").replace("_kernel", ""),
    )(page_table, *inputs)
    return out.reshape(nb, HD)


def _suffix_matrix_f(span):
    r = np.arange(span)
    return jnp.asarray((r[:, None] > r[None, :]).astype(np.float32), BF16)


def _fox_decode_paged(q, kv_new, logf_new, cache_kv, cache_lf, page_table, layer):
    nb = q.shape[0]
    page = cache_kv.shape[2]
    span = DEC_PAGES_PER_STEP * page
    seg = _head_segments(HD, N_HEADS, lambda hh: hh)
    row = lambda shape: pl.BlockSpec(shape, lambda b, g, pt: (b,) + (0,) * (len(shape) - 1))
    const = lambda shape: pl.BlockSpec(shape, lambda b, g, pt: (0,) * len(shape))
    lead_inputs = [q.reshape(nb, 1, HD), kv_new.reshape(nb, 1, 2 * HD), kv_new[:, HD:].reshape(nb, N_HEADS, HEAD_DIM),
                   jnp.broadcast_to(logf_new.reshape(nb, N_HEADS, 1), (nb, N_HEADS, LANE)),
                   jnp.asarray(seg, BF16), jnp.ones((N_HEADS, LANE), BF16), jnp.ones((8, page), BF16), _suffix_matrix_f(span)]
    lead_specs = [row((1, 1, HD)), row((1, 1, 2 * HD)), row((1, N_HEADS, HEAD_DIM)), row((1, N_HEADS, LANE)),
                  const((HD, N_HEADS)), const((N_HEADS, LANE)), const((8, page)), const((span, span))]
    return _paged_decode(_fox_dect_kernel, cache_kv, page_table, layer, q, lead_inputs, lead_specs, cache_lf=cache_lf,
                         scratch=[pltpu.VMEM((N_HEADS, LANE), F32)] * 3)


def _sb_decode_paged(q, cache_kv, page_table, layer):
    page = cache_kv.shape[2]
    span = DEC_PAGES_PER_STEP * page
    const = lambda shape: pl.BlockSpec(shape, lambda b, g, pt: (0,) * len(shape))
    return _paged_decode(_sb_dect_kernel, cache_kv, page_table, layer, q,
                         [jnp.ones((8, page), BF16), _suffix_matrix_f(span)], [const((8, page)), const((span, span))],
                         scratch=[pltpu.VMEM((N_HEADS, LANE), F32)])


def _dec_branch(k, v, qg, segs, segts, mask, k_new, v_new):
    s = None
    for g in range(NSA_GROUP):
        part = _dot((k * qg[g]).astype(BF16), segs[g])
        s = part if s is None else s + part
    if mask is not None:
        s = jnp.where(mask, s, NEG_INF)
    m = jnp.max(s, axis=0, keepdims=True)
    if k_new is not None:
        s_n = None
        for g in range(NSA_GROUP):
            part = _dot(_row8((k_new * qg[g]).astype(BF16)), segs[g])[0:1]
            s_n = part if s_n is None else s_n + part
        m = jnp.maximum(m, s_n)
    e = jnp.exp(s - m)
    if mask is not None:
        e = jnp.where(mask, e, 0.0)
    l = jnp.sum(e, axis=0, keepdims=True)
    if k_new is not None:
        e_n = jnp.exp(s_n - m)
        l = l + e_n
    inv = 1.0 / l
    eb = e.astype(BF16)
    outs = []
    for g in range(NSA_GROUP):
        o = jnp.sum(_dot(eb, segts[g]) * v, axis=0, keepdims=True)
        if k_new is not None:
            o = o + _row_expand(e_n, segts[g]) * v_new
        outs.append(o * _row_expand(inv, segts[g]))
    return outs, e * inv


def _nsa_dec_kernel(q_ref, gx_ref, new_ref, wnew_ref, cmp_ref, ks_ref, vs_ref, win_ref,
                    seg_ref, segt_ref, rep_ref, erep_ref, o_ref, *, n_blk):
    segs = [seg_ref[g] for g in range(NSA_GROUP)]
    segts = [segt_ref[g] for g in range(NSA_GROUP)]
    qg = [q_ref[0, :, g * NSA_KVD:(g + 1) * NSA_KVD] for g in range(NSA_GROUP)]
    ncol = NSA_GROUP * NSA_KV_HEADS

    o_c, p_c = _dec_branch(cmp_ref[0, 0], cmp_ref[0, 1], qg, segs, segts, None, None, None)

    imp = _dot_x3(p_c, rep_ref[...])
    nc = imp.shape[0]
    pad = erep_ref.shape[1] - nc
    imp = jnp.concatenate([imp, jnp.zeros((pad, ncol), F32)], axis=0)
    blk = lax.broadcasted_iota(jnp.int32, imp.shape, 0)
    cur = n_blk - 1
    valid = blk <= cur
    forced = (blk == 0) | (blk == cur) | (blk == cur - 1)
    score = jnp.where(forced, FORCE_SCORE, jnp.where(valid, imp, NEG_INF))
    rank = jnp.zeros(imp.shape, jnp.int32)
    for j in range(n_blk):
        rj = score[j:j + 1, :]
        rank = rank + ((rj > score) | ((rj == score) & (j < blk))).astype(jnp.int32)
    sel = jnp.where((rank < TOP_N) & valid, 1.0, 0.0).astype(BF16)

    in_sel = _dot(erep_ref[...], sel) > 0.5
    k_new = new_ref[0, :, 2 * NSA_KVD:3 * NSA_KVD]
    v_new = new_ref[0, :, 3 * NSA_KVD:4 * NSA_KVD]
    o_s, _ = _dec_branch(ks_ref[0], vs_ref[0], qg, segs, segts, in_sel, k_new, v_new)

    o_w, _ = _dec_branch(win_ref[0, :, 0:NSA_KVD], win_ref[0, :, NSA_KVD:2 * NSA_KVD], qg, segs, segts, None,
                         wnew_ref[0, :, 0:NSA_KVD], wnew_ref[0, :, NSA_KVD:2 * NSA_KVD])

    for g in range(NSA_GROUP):
        sl = slice(g * NSA_KVD, (g + 1) * NSA_KVD)
        o_ref[0, :, sl] = gx_ref[0, 0:1, sl] * o_c[g] + gx_ref[0, 1:2, sl] * o_s[g] + gx_ref[0, 2:3, sl] * o_w[g]


def _group_major(x):
    lead = x.shape[:-1]
    y = x.reshape(lead + (NSA_KV_HEADS, NSA_GROUP, HEAD_DIM))
    return jnp.swapaxes(y, -3, -2).reshape(lead + (HD,))


def _nsa_decode(q, gates, rows_new, win_new, cmp_kv, rows_past, win_buf):
    nb, length, _ = rows_past.shape
    wlen = win_buf.shape[1]
    nc = length // CMP_BLOCK
    n_blk = -(-(length + 1) // SEL_BLOCK)
    n_blk_pad = -(-n_blk // 8) * 8
    ncol = NSA_GROUP * NSA_KV_HEADS
    segs = np.stack([_head_segments(NSA_KVD, ncol, lambda hh, g=g: NSA_KV_HEADS * g + hh) for g in range(NSA_GROUP)])
    rep = (np.arange(ncol)[:, None] % NSA_KV_HEADS == np.arange(ncol)[None, :] % NSA_KV_HEADS).astype(np.float32)
    erep = (np.arange(length)[:, None] // SEL_BLOCK == np.arange(n_blk_pad)[None, :]).astype(np.float32)
    qg = _group_major(q)
    gx = gates[:, :N_HEADS * 3].reshape(nb, N_HEADS, 3)
    gx = jnp.broadcast_to(jnp.swapaxes(gx, 1, 2)[..., None], (nb, 3, N_HEADS, HEAD_DIM))
    gx = _group_major(gx.reshape(nb, 3, HD))
    out = pl.pallas_call(
        functools.partial(_nsa_dec_kernel, n_blk=n_blk),
        grid=(nb,),
        in_specs=[
            pl.BlockSpec((1, 1, HD), lambda b: (b, 0, 0)),
            pl.BlockSpec((1, 3, HD), lambda b: (b, 0, 0)),
            pl.BlockSpec((1, 1, 4 * NSA_KVD), lambda b: (b, 0, 0)),
            pl.BlockSpec((1, 1, 2 * NSA_KVD), lambda b: (b, 0, 0)),
            pl.BlockSpec((1, 2, nc, NSA_KVD), lambda b: (b, 0, 0, 0)),
            pl.BlockSpec((1, length, NSA_KVD), lambda b: (b, 0, 2)),
            pl.BlockSpec((1, length, NSA_KVD), lambda b: (b, 0, 3)),
            pl.BlockSpec((1, wlen, 2 * NSA_KVD), lambda b: (b, 0, 0)),
            _const_spec((NSA_GROUP, NSA_KVD, ncol)), _const_spec((NSA_GROUP, ncol, NSA_KVD)),
            _const_spec((ncol, ncol)), _const_spec((length, n_blk_pad)),
        ],
        out_specs=pl.BlockSpec((1, 1, HD), lambda b: (b, 0, 0)),
        out_shape=jax.ShapeDtypeStruct((nb, 1, HD), F32),
        compiler_params=_cp(("parallel",)),
        name="nsa_decode",
    )(qg.reshape(nb, 1, HD), gx, rows_new.reshape(nb, 1, 4 * NSA_KVD), win_new.reshape(nb, 1, 2 * NSA_KVD), cmp_kv,
      rows_past, rows_past, win_buf, jnp.asarray(segs, BF16), jnp.asarray(segs.transpose(0, 2, 1), BF16),
      jnp.asarray(rep, BF16), jnp.asarray(erep, BF16))
    o = out.reshape(nb, NSA_GROUP, NSA_KV_HEADS, HEAD_DIM)
    return jnp.swapaxes(o, 1, 2).reshape(nb, HD)


def kernel(x_prompt, x_sample, cache_nsa_kv, state_nsa_win, cache_fox_kv, cache_fox_logf, cache_sb_kv, page_table,
           ffn_norm, ffn_w_gate, ffn_w_up, ffn_w_down, mix_norm, nsa_w_in, nsa_q_gain, nsa_k_gain, nsa_cmp_pos,
           nsa_cmp_w1, nsa_cmp_w2, nsa_w_out, fox_w_in, fox_b_f, fox_q_gain, fox_k_gain, fox_w_out, sb_w_in, sb_w_out):
    n, s, d = x_prompt.shape
    nb, dec_seq, _ = x_sample.shape
    assert dec_seq == 1 and d == D_MODEL
    depth = ffn_norm.shape[0]
    page = cache_nsa_kv.shape[2]
    past = page_table.shape[1] * page
    xp = x_prompt.reshape(n * s, d)
    xs = x_sample.reshape(nb, d)
    wg, wu, wd = ffn_w_gate.astype(BF16), ffn_w_up.astype(BF16), ffn_w_down.astype(BF16)
    pos_p = jnp.arange(s, dtype=jnp.int32)
    pos_s = jnp.full((1,), past, jnp.int32)
    out = {k: [] for k in ("nsa_kv_p", "nsa_win_p", "fox_kv_p", "fox_lf_p", "sb_kv_p",
                           "nsa_kv_s", "nsa_win_s", "fox_kv_s", "fox_lf_s", "sb_kv_s")}
    for i in range(depth):
        j = i // 3
        xp = _ffn_half(xp, ffn_norm[i, 0], wg[i, 0], wu[i, 0], wd[i, 0])
        xs = _ffn_half(xs, ffn_norm[i, 0], wg[i, 0], wu[i, 0], wd[i, 0])
        if i % 3 == 0:
            qg, kg = nsa_q_gain[j], nsa_k_gain[j]
            w_out = nsa_w_out[j].astype(BF16)
            cmp_w = _cmp_weights(nsa_cmp_pos[j], nsa_cmp_w1[j], nsa_cmp_w2[j])
            q, rows, win, kvb, gates = _nsa_project(xp, mix_norm[i], nsa_w_in[j], qg, kg, pos_p, BF16)
            cmp_kv = _nsa_compress(rows.reshape(n, s, 4 * NSA_KVD), n, s, 1, cmp_w, BF16)
            xp = _out_proj(xp, _nsa_attention(q, gates, kvb, cmp_kv, n, s), w_out)
            out["nsa_kv_p"].append(rows.reshape(n, s, 4, NSA_KV_HEADS, HEAD_DIM))
            out["nsa_win_p"].append(win.reshape(n, s, 2, NSA_KV_HEADS, HEAD_DIM)[:, s - min(WINDOW, s):])

            q, rows, win, _, gates = _nsa_project(xs, mix_norm[i], nsa_w_in[j], qg, kg, pos_s, F32)
            rows_past = _gather_pages(cache_nsa_kv, page_table, j)
            cmp_kv = _nsa_compress(rows_past, nb, past, 16 if nb % 16 == 0 else 1, cmp_w, F32)
            wlen = state_nsa_win.shape[2]
            win_buf = state_nsa_win[j].reshape(nb, wlen, 2 * NSA_KVD)
            xs = _out_proj(xs, _nsa_decode(q, gates, rows, win, cmp_kv, rows_past, win_buf), w_out)
            out["nsa_kv_s"].append(rows.reshape(nb, 1, 4, NSA_KV_HEADS, HEAD_DIM))
            win_all = jnp.concatenate([win_buf, win.reshape(nb, 1, 2 * NSA_KVD)], axis=1)[:, 1:]
            out["nsa_win_s"].append(win_all.reshape(nb, wlen, 2, NSA_KV_HEADS, HEAD_DIM))
        elif i % 3 == 1:
            qg, kg = fox_q_gain[j], fox_k_gain[j]
            w_out = fox_w_out[j].astype(BF16)
            q, kv, kvb, logf, c = _fox_project(xp, mix_norm[i], fox_w_in[j], fox_b_f[j], qg, kg, s, BF16)
            xp = _out_proj(xp, _fox_attention(q, kvb, c, n, s), w_out)
            out["fox_kv_p"].append(kv.reshape(n, s, 2, N_HEADS, HEAD_DIM))
            out["fox_lf_p"].append(logf[:, :N_HEADS].reshape(n, s, N_HEADS))

            q, kv, _, logf, _ = _fox_project(xs, mix_norm[i], fox_w_in[j], fox_b_f[j], qg, kg, 1, F32)
            o = _fox_decode_paged(q, kv, logf[:, :N_HEADS], cache_fox_kv, cache_fox_logf, page_table, j)
            xs = _out_proj(xs, o, w_out)
            out["fox_kv_s"].append(kv.reshape(nb, 1, 2, N_HEADS, HEAD_DIM))
            out["fox_lf_s"].append(logf[:, :N_HEADS].reshape(nb, 1, N_HEADS))
        else:
            w_out = sb_w_out[j].astype(BF16)
            q, kv, kvb = _sb_project(xp, mix_norm[i], sb_w_in[j], BF16)
            xp = _out_proj(xp, _sb_attention(q, kvb, n, s), w_out)
            out["sb_kv_p"].append(kv.reshape(n, s, 2, N_HEADS, HEAD_DIM))

            q, kv, _ = _sb_project(xs, mix_norm[i], sb_w_in[j], F32)
            xs = _out_proj(xs, _sb_decode_paged(q, cache_sb_kv, page_table, j), w_out)
            out["sb_kv_s"].append(kv.reshape(nb, 1, 2, N_HEADS, HEAD_DIM))
        xp = _ffn_half(xp, ffn_norm[i, 1], wg[i, 1], wu[i, 1], wd[i, 1])
        xs = _ffn_half(xs, ffn_norm[i, 1], wg[i, 1], wu[i, 1], wd[i, 1])
    st = {k: jnp.stack(v) for k, v in out.items()}
    return (xp.reshape(n, s, d), xs.reshape(nb, 1, d),
            st["nsa_kv_p"], st["nsa_win_p"], st["fox_kv_p"], st["fox_lf_p"], st["sb_kv_p"],
            st["nsa_kv_s"], st["nsa_win_s"], st["fox_kv_s"], st["fox_lf_s"], st["sb_kv_s"])
```
